```python
import math
import jax
import jax.numpy as jnp
from jax import lax
import numpy as np

D_MODEL = 1024
BATCH = 8
SEQ = 2048
DEPTH = 1

CTX_LEN = 256
GRID_W = 64

ATTN_HEADS = 8
ATTN_KV_HEADS = 2
ATTN_GROUP = ATTN_HEADS // ATTN_KV_HEADS
HEAD_DIM = 128
ROPE_AXIS_DIM = HEAD_DIM // 2
ROPE_THETA = 10000.0
Q_BLOCK = 128

GDN_HEADS = 8
GDN_DK = 128
GDN_DV = 128
GDN_CHUNK = 64
SHORT_CONV = 3

D_FF = 2816
FFN_CONV = 3

NORM_EPS = 1e-6

ATTN_Q_W = ATTN_HEADS * HEAD_DIM
ATTN_KV_W = ATTN_KV_HEADS * HEAD_DIM
GDN_QK_W = GDN_HEADS * GDN_DK
GDN_V_W = GDN_HEADS * GDN_DV
GDN_CONV_W = 2 * GDN_QK_W + GDN_V_W
IN_SPLITS = (ATTN_KV_W, ATTN_KV_W, GDN_CONV_W, 2 * GDN_HEADS, 2 * GDN_HEADS, ATTN_Q_W, GDN_V_W, 2 * D_MODEL)
CTX_COLS = 2 * ATTN_KV_W + GDN_CONV_W + 4 * GDN_HEADS
IN_COLS = CTX_COLS + ATTN_Q_W + GDN_V_W + 2 * D_MODEL

kernel_name = 'hybrid_gqa_gdn_convffn_prefix_block'


def rms_norm(x, gain=None):
    x32 = x.astype(jnp.float32)
    y = x32 * lax.rsqrt(jnp.mean(x32 * x32, axis=-1, keepdims=True) + NORM_EPS)
    if gain is not None:
        y = y * gain.astype(jnp.float32)
    return y.astype(x.dtype)


def l2_normalize(x):
    return x * lax.rsqrt(jnp.sum(x * x, axis=-1, keepdims=True) + NORM_EPS)


def modulate(h, shift, scale):
    return h * (1.0 + scale) + shift


def split_cols(p):
    parts, off = [], 0
    for size in IN_SPLITS:
        if off >= p.shape[-1]:
            break
        parts.append(p[..., off:off + size])
        off += size
    return parts


def dwconv_centred(x, w, b=None):
    width = w.shape[0]
    pad = width // 2
    n = x.shape[1]
    xp = jnp.pad(x, ((0, 0), (pad, pad), (0, 0)))
    y = xp[:, 0:n] * w[0]
    for j in range(1, width):
        y = y + xp[:, j:j + n] * w[j]
    return y if b is None else y + b


def axial_rope_tables(n):
    rows = n // GRID_W
    row_ids = jnp.broadcast_to(jnp.arange(rows, dtype=jnp.float32)[:, None], (rows, GRID_W)).reshape(n)
    col_ids = jnp.broadcast_to(jnp.arange(GRID_W, dtype=jnp.float32)[None, :], (rows, GRID_W)).reshape(n)
    inv_freq = ROPE_THETA ** (-jnp.arange(0, ROPE_AXIS_DIM, 2, dtype=jnp.float32) / ROPE_AXIS_DIM)
    ang = jnp.concatenate([row_ids[:, None] * inv_freq, col_ids[:, None] * inv_freq], axis=-1)
    ang = ang.reshape(n, 2, ROPE_AXIS_DIM // 2)
    return jnp.cos(ang), jnp.sin(ang)


def apply_axial_rope(x, cos, sin):
    x32 = x.astype(jnp.float32).reshape(*x.shape[:-1], 2, 2, ROPE_AXIS_DIM // 2)
    x1, x2 = x32[..., 0, :], x32[..., 1, :]
    cos = cos[None, :, None]
    sin = sin[None, :, None]
    out = jnp.stack([x1 * cos - x2 * sin, x2 * cos + x1 * sin], axis=-2)
    return out.reshape(x.shape).astype(x.dtype)


def sdpa_block(q, k, v):
    s = jnp.einsum('bqhgd,bkhd->bhgqk', q, k).astype(jnp.float32) * (HEAD_DIM ** -0.5)
    p = jax.nn.softmax(s, axis=-1).astype(v.dtype)
    return jnp.einsum('bhgqk,bkhd->bqhgd', p, v)


def latent_attention(q, k_all, v_all):
    b, n = q.shape[:2]
    nb = n // Q_BLOCK
    qb = jnp.moveaxis(q.reshape(b, nb, Q_BLOCK, *q.shape[2:]), 1, 0)
    o = lax.map(lambda q_blk: sdpa_block(q_blk, k_all, v_all), qb)
    return jnp.moveaxis(o, 0, 1).reshape(b, n, ATTN_Q_W)


def gdn_chunked(q, k, v, log_a, beta, s0, with_output):
    b, h, n, dk = q.shape
    dv = v.shape[-1]
    nc = n // GDN_CHUNK
    q = q.reshape(b, h, nc, GDN_CHUNK, dk)
    k = k.reshape(b, h, nc, GDN_CHUNK, dk)
    v = v.reshape(b, h, nc, GDN_CHUNK, dv)
    log_a = log_a.reshape(b, h, nc, GDN_CHUNK)
    beta = beta.reshape(b, h, nc, GDN_CHUNK)
    gam = jnp.cumsum(log_a, axis=-1)
    idx = jnp.arange(GDN_CHUNK)
    strict = idx[:, None] > idx[None, :]
    incl = idx[:, None] >= idx[None, :]
    dec = jnp.exp(jnp.where(incl, gam[..., :, None] - gam[..., None, :], -jnp.inf))
    kk = jnp.einsum('bhncd,bhnsd->bhncs', k, k)
    a_mat = jnp.where(strict, beta[..., :, None] * dec * kk, 0.0) + jnp.eye(GDN_CHUNK, dtype=jnp.float32)
    rhs = jnp.concatenate([beta[..., None] * v, (beta * jnp.exp(gam))[..., None] * k], axis=-1)
    uw = lax.linalg.triangular_solve(a_mat, rhs, left_side=True, lower=True, unit_diagonal=True)
    u, w = uw[..., :dv], uw[..., dv:]
    k_dec = k * jnp.exp(gam[..., -1:] - gam)[..., None]
    g_last = jnp.exp(gam[..., -1])
    xs = [u, w, k_dec, g_last]
    if with_output:
        q_dec = q * jnp.exp(gam)[..., None]
        p = dec * jnp.einsum('bhncd,bhnsd->bhncs', q, k)
        xs = xs + [q_dec, p]
    xs = tuple(jnp.moveaxis(t, 2, 0) for t in xs)

    def step(s, inp):
        u_c, w_c, kd_c, gl_c = inp[:4]
        delta = u_c - jnp.einsum('bhcd,bhde->bhce', w_c, s)
        s_new = gl_c[..., None, None] * s + jnp.einsum('bhcd,bhce->bhde', kd_c, delta)
        if with_output:
            qd_c, p_c = inp[4], inp[5]
            o = jnp.einsum('bhcd,bhde->bhce', qd_c, s) + jnp.einsum('bhcs,bhse->bhce', p_c, delta)
            return s_new, o
        return s_new, None

    s_fin, o = lax.scan(step, s0, xs)
    if with_output:
        o = jnp.moveaxis(o, 0, 2).reshape(b, h, n, dv)
    return o, s_fin


def gdn_prepare(qkv, db, da, conv_w, a_log, dt_bias):
    b, n, _ = qkv.shape
    qkv = jax.nn.silu(dwconv_centred(qkv, conv_w)).astype(jnp.float32)
    q, k, v = jnp.split(qkv, [GDN_QK_W, 2 * GDN_QK_W], axis=-1)
    q = l2_normalize(q.reshape(b, n, GDN_HEADS, GDN_DK)) * (GDN_DK ** -0.5)
    k = l2_normalize(k.reshape(b, n, GDN_HEADS, GDN_DK))
    v = v.reshape(b, n, GDN_HEADS, GDN_DV)
    q, k, v = (t.transpose(0, 2, 1, 3) for t in (q, k, v))
    beta = jax.nn.sigmoid(db.astype(jnp.float32)).reshape(b, n, 2, GDN_HEADS).transpose(2, 0, 3, 1)
    da = da.astype(jnp.float32).reshape(b, n, 2, GDN_HEADS).transpose(2, 0, 3, 1)
    log_a = -jnp.exp(a_log.astype(jnp.float32))[:, None, :, None] * jax.nn.softplus(
        da + dt_bias.astype(jnp.float32)[:, None, :, None])
    return q, k, v, log_a, beta


def gdn_bidirectional(ctx_in, lat_in, with_ctx_output):
    qc, kc, vc, lac, bc = ctx_in
    ql, kl, vl, lal, bl = lat_in
    s0 = jnp.zeros((ql.shape[0], GDN_HEADS, GDN_DK, GDN_DV), jnp.float32)
    o_lat, o_ctx = None, None
    for d in range(2):
        rev = (lambda t: jnp.flip(t, axis=2)) if d == 1 else (lambda t: t)
        oc, sc = gdn_chunked(rev(qc), rev(kc), rev(vc), rev(lac[d]), rev(bc[d]), s0, with_ctx_output)
        ol, _ = gdn_chunked(rev(ql), rev(kl), rev(vl), rev(lal[d]), rev(bl[d]), sc, True)
        o_lat = rev(ol) if o_lat is None else o_lat + rev(ol)
        if with_ctx_output:
            o_ctx = rev(oc) if o_ctx is None else o_ctx + rev(oc)
    return o_lat, o_ctx


def gdn_output(o, z, norm_w):
    b, n = z.shape[:2]
    o = rms_norm(o.transpose(0, 2, 1, 3), norm_w)
    y = o * jax.nn.silu(z.astype(jnp.float32).reshape(b, n, GDN_HEADS, GDN_DV))
    return y.reshape(b, n, GDN_V_W).astype(z.dtype)


def merge_branches(attn, gdn, gates, w_pa, w_pd, w_out):
    g_a, g_d = jnp.split(gates, 2, axis=-1)
    y = jax.nn.sigmoid(g_a) * (attn @ w_pa) + jax.nn.sigmoid(g_d) * (gdn @ w_pd)
    return y @ w_out


def conv_ffn(h, w_up, conv_w, conv_b, w_down):
    u = dwconv_centred(h @ w_up, conv_w, conv_b)
    g, val = jnp.split(u, 2, axis=-1)
    return (jax.nn.silu(g) * val) @ w_down


def hybrid_layer(x, ctx, c, c_ctx, w_mod, b_mod, w_in, q_norm_w, k_norm_w, conv_qkv_w, a_log, dt_bias,
                 gdn_norm_w, w_pa, w_pd, w_out, w_up, ffn_conv_w, ffn_conv_b, w_down, update_ctx):
    b, n, _ = x.shape
    cl = ctx.shape[1]
    mod_lat = jax.nn.silu(c) @ w_mod + b_mod
    mod_ctx = jax.nn.silu(c_ctx) @ w_mod + b_mod
    sh1, sc1, g1, sh2, sc2, g2 = [m[:, None, :] for m in jnp.split(mod_lat, 6, axis=-1)]
    csh1, csc1, cg1, csh2, csc2, cg2 = jnp.split(mod_ctx, 6, axis=-1)

    hx = modulate(rms_norm(x), sh1, sc1)
    hc = modulate(rms_norm(ctx), csh1, csc1)
    ak_x, av_x, qkv_x, db_x, da_x, aq_x, z_x, gate_x = split_cols(hx @ w_in)
    ak_c, av_c, qkv_c, db_c, da_c, *rest_c = split_cols(hc @ (w_in if update_ctx else w_in[:, :CTX_COLS]))

    cos, sin = axial_rope_tables(n)
    q_x = apply_axial_rope(rms_norm(aq_x.reshape(b, n, ATTN_HEADS, HEAD_DIM), q_norm_w), cos, sin)
    q_x = q_x.reshape(b, n, ATTN_KV_HEADS, ATTN_GROUP, HEAD_DIM)
    k_x = apply_axial_rope(rms_norm(ak_x.reshape(b, n, ATTN_KV_HEADS, HEAD_DIM), k_norm_w), cos, sin)
    v_x = av_x.reshape(b, n, ATTN_KV_HEADS, HEAD_DIM)
    k_c = rms_norm(ak_c.reshape(b, cl, ATTN_KV_HEADS, HEAD_DIM), k_norm_w)
    v_c = av_c.reshape(b, cl, ATTN_KV_HEADS, HEAD_DIM)
    attn_x = latent_attention(q_x, jnp.concatenate([k_c, k_x], axis=1), jnp.concatenate([v_c, v_x], axis=1))

    gdn_x_in = gdn_prepare(qkv_x, db_x, da_x, conv_qkv_w, a_log, dt_bias)
    gdn_c_in = gdn_prepare(qkv_c, db_c, da_c, conv_qkv_w, a_log, dt_bias)
    o_x, o_c = gdn_bidirectional(gdn_c_in, gdn_x_in, update_ctx)
    gdn_x = gdn_output(o_x, z_x, gdn_norm_w)

    x = x + g1 * merge_branches(attn_x, gdn_x, gate_x, w_pa, w_pd, w_out)
    x = x + g2 * conv_ffn(modulate(rms_norm(x), sh2, sc2), w_up, ffn_conv_w, ffn_conv_b, w_down)

    if update_ctx:
        aq_c, z_c, gate_c = rest_c
        q_c = rms_norm(aq_c.reshape(b, cl, ATTN_HEADS, HEAD_DIM), q_norm_w)
        q_c = q_c.reshape(b, cl, ATTN_KV_HEADS, ATTN_GROUP, HEAD_DIM)
        attn_c = sdpa_block(q_c, k_c, v_c).reshape(b, cl, ATTN_Q_W)
        gdn_c = gdn_output(o_c, z_c, gdn_norm_w)
        ctx = ctx + cg1 * merge_branches(attn_c, gdn_c, gate_c, w_pa, w_pd, w_out)
        ctx = ctx + cg2 * conv_ffn(modulate(rms_norm(ctx), csh2, csc2), w_up, ffn_conv_w, ffn_conv_b, w_down)
    return x, ctx


def setup_inputs(seed: int = 0) -> dict:
    key = jax.random.key(seed)
    ks = jax.random.split(key, 22)
    f32 = jnp.float32

    def dense(k, shape, fan_in, s=1.0):
        return s * (fan_in ** -0.5) * jax.random.normal(k, shape, f32)

    dt = jnp.exp(jax.random.uniform(ks[10], (DEPTH, 2, GDN_HEADS), f32,
                                    minval=math.log(1e-3), maxval=math.log(1e-1)))
    return {
        'x': jax.random.normal(ks[0], (BATCH, SEQ, D_MODEL), f32),
        'c': jax.random.normal(ks[1], (BATCH, D_MODEL), f32),
        'ctx': jax.random.normal(ks[2], (BATCH, CTX_LEN, D_MODEL), f32),
        'c_ctx': jax.random.normal(ks[3], (D_MODEL,), f32),
        'w_mod': dense(ks[4], (DEPTH, D_MODEL, 6 * D_MODEL), D_MODEL, 0.5),
        'b_mod': 0.02 * jax.random.normal(ks[5], (DEPTH, 6 * D_MODEL), f32),
        'w_in': dense(ks[6], (DEPTH, D_MODEL, IN_COLS), D_MODEL),
        'q_norm_w': 1.0 + 0.05 * jax.random.normal(ks[7], (DEPTH, HEAD_DIM), f32),
        'k_norm_w': 1.0 + 0.05 * jax.random.normal(ks[8], (DEPTH, HEAD_DIM), f32),
        'conv_qkv_w': dense(ks[9], (DEPTH, SHORT_CONV, GDN_CONV_W), SHORT_CONV),
        'a_log': jnp.log(jax.random.uniform(ks[11], (DEPTH, 2, GDN_HEADS), f32, minval=1.0, maxval=16.0)),
        'dt_bias': dt + jnp.log(-jnp.expm1(-dt)),
        'gdn_norm_w': 1.0 + 0.05 * jax.random.normal(ks[12], (DEPTH, GDN_DV), f32),
        'w_pa': dense(ks[13], (DEPTH, ATTN_Q_W, D_MODEL), ATTN_Q_W),
        'w_pd': dense(ks[14], (DEPTH, GDN_V_W, D_MODEL), GDN_V_W),
        'w_out': dense(ks[15], (DEPTH, D_MODEL, D_MODEL), D_MODEL),
        'w_up': dense(ks[16], (DEPTH, D_MODEL, 2 * D_FF), D_MODEL),
        'ffn_conv_w': dense(ks[17], (DEPTH, FFN_CONV, 2 * D_FF), FFN_CONV),
        'ffn_conv_b': 0.02 * jax.random.normal(ks[18], (DEPTH, 2 * D_FF), f32),
        'w_down': dense(ks[19], (DEPTH, D_FF, D_MODEL), D_FF),
        'final_norm_w': 1.0 + 0.05 * jax.random.normal(ks[20], (D_MODEL,), f32),
    }


def reference(x, c, ctx, c_ctx, w_mod, b_mod, w_in, q_norm_w, k_norm_w, conv_qkv_w, a_log, dt_bias,
              gdn_norm_w, w_pa, w_pd, w_out, w_up, ffn_conv_w, ffn_conv_b, w_down, final_norm_w):
    for layer in range(DEPTH):
        x, ctx = hybrid_layer(
            x, ctx, c, c_ctx, w_mod[layer], b_mod[layer], w_in[layer], q_norm_w[layer], k_norm_w[layer],
            conv_qkv_w[layer], a_log[layer], dt_bias[layer], gdn_norm_w[layer], w_pa[layer], w_pd[layer],
            w_out[layer], w_up[layer], ffn_conv_w[layer], ffn_conv_b[layer], w_down[layer],
            update_ctx=layer < DEPTH - 1)
    return rms_norm(x, final_norm_w)
```

```python
import functools
import math

import jax
import jax.numpy as jnp
from jax import lax
from jax.experimental import pallas as pl
from jax.experimental.pallas import tpu as pltpu

F32 = jnp.float32
BF16 = jnp.bfloat16

LANES = 128
GRID_W = 64
ATTN_HEADS = 8
ATTN_KV_HEADS = 2
ATTN_GROUP = ATTN_HEADS // ATTN_KV_HEADS
HEAD_DIM = 128
ROPE_AXIS_DIM = HEAD_DIM // 2
ROPE_THETA = 10000.0
GDN_HEADS = 8
GDN_DK = 128
GDN_DV = 128
GDN_CHUNK = 128
NORM_EPS = 1e-6
NEG_BIG = -1e30

PROJ_TILE = 512
COL_GATES = 0
COL_AQ = 2048
COL_Z = 3072
COL_GQ = 4096
COL_GK = 5120
COL_GV = 6144
COL_AK = 7168
COL_AV = 7424
MAIN_COLS = 7680
CTX_COL0 = COL_GQ
VMEM_LIMIT = 56 * 1024 * 1024


def _dot(a, b):
    return jnp.dot(a, b, preferred_element_type=F32)


def _dot_nt(a, b):
    return lax.dot_general(a, b, (((1,), (1,)), ((), ())), preferred_element_type=F32)


def _sigmoid(x):
    return 1.0 / (1.0 + jnp.exp(-x))


def _silu(x):
    return x * _sigmoid(x)


def _mod_kernel(c_ref, w_ref, b_ref, o_ref):
    s = _silu(c_ref[...])
    w = w_ref[...]
    s_hi = s.astype(BF16)
    s_lo = (s - s_hi.astype(F32)).astype(BF16)
    w_hi = w.astype(BF16)
    w_lo = (w - w_hi.astype(F32)).astype(BF16)
    o_ref[...] = _dot(s_hi, w_hi) + _dot(s_lo, w_hi) + _dot(s_hi, w_lo) + b_ref[...]


def _modulation(cc, w_mod, b_mod):
    rows, d = cc.shape
    n = w_mod.shape[1]
    tn = 1024
    return pl.pallas_call(
        _mod_kernel,
        grid=(n // tn,),
        in_specs=[pl.BlockSpec((rows, d), lambda j: (0, 0)),
                  pl.BlockSpec((d, tn), lambda j: (0, j)),
                  pl.BlockSpec((1, tn), lambda j: (0, j))],
        out_specs=pl.BlockSpec((rows, tn), lambda j: (0, j)),
        out_shape=jax.ShapeDtypeStruct((rows, n), F32),
    )(cc, w_mod, b_mod)


def _shift_rows(a, seq_len):
    tm = a.shape[0]
    t = lax.broadcasted_iota(jnp.int32, (tm, 1), 0) % seq_len
    prev = jnp.where(t == 0, 0.0, pltpu.roll(a, 1, axis=0))
    nxt = jnp.where(t == seq_len - 1, 0.0, pltpu.roll(a, tm - 1, axis=0))
    return prev, nxt


def _conv3(a, w_ref, seq_len):
    prev, nxt = _shift_rows(a, seq_len)
    return prev * w_ref[0:1, :] + a * w_ref[1:2, :] + nxt * w_ref[2:3, :]


def _rope(a, cos, sin):
    lane = lax.broadcasted_iota(jnp.int32, a.shape, 1)
    swapped = jnp.where((lane & 32) == 0, pltpu.roll(a, 96, axis=1), pltpu.roll(a, 32, axis=1))
    return a * cos + swapped * sin


def _inproj_kernel(x_ref, sh_ref, sc_ref, w_ref, ws_ref, qn_ref, kn_ref, cw_ref, cos_ref, sin_ref,
                   o_ref, os_ref, hx_ref, *, tile0, ntiles, seq_len, rope):
    j = pl.program_id(1)

    @pl.when(j == 0)
    def _():
        x = x_ref[...]
        r = lax.rsqrt(jnp.mean(x * x, axis=-1, keepdims=True) + NORM_EPS)
        h = (x * r) * (1.0 + sc_ref[0]) + sh_ref[0]
        hb = h.astype(BF16)
        hx_ref[...] = hb
        os_ref[...] = _dot(hb, ws_ref[...])

    acc = _dot(hx_ref[...], w_ref[...])
    kind = j + tile0
    nh = PROJ_TILE // LANES

    def reachable(lo, hi):
        return max(lo, tile0) < min(hi, tile0 + ntiles)

    def qk_norm_rope(a, w, scale):
        a = a * lax.rsqrt(jnp.mean(a * a, axis=-1, keepdims=True) + NORM_EPS) * w
        if rope:
            a = _rope(a, cos_ref[...], sin_ref[...])
        return a * scale if scale != 1.0 else a

    if reachable(0, 4):
        @pl.when(kind < 4)
        def _():
            o_ref[...] = _sigmoid(acc).astype(BF16)

    if reachable(4, 6):
        @pl.when((kind >= 4) & (kind < 6))
        def _():
            for hh in range(nh):
                a = acc[:, hh * LANES:(hh + 1) * LANES]
                o_ref[:, hh * LANES:(hh + 1) * LANES] = qk_norm_rope(a, qn_ref[...], HEAD_DIM ** -0.5).astype(BF16)

    if reachable(6, 8):
        @pl.when((kind >= 6) & (kind < 8))
        def _():
            o_ref[...] = acc.astype(BF16)

    if reachable(8, 12):
        @pl.when((kind >= 8) & (kind < 12))
        def _():
            y = _silu(_conv3(acc, cw_ref, seq_len))
            scale = jnp.where(kind < 10, GDN_DK ** -0.5, 1.0).astype(F32)
            for hh in range(nh):
                a = y[:, hh * LANES:(hh + 1) * LANES]
                a = a * (lax.rsqrt(jnp.sum(a * a, axis=-1, keepdims=True) + NORM_EPS) * scale)
                o_ref[:, hh * LANES:(hh + 1) * LANES] = a.astype(BF16)

    if reachable(12, 14):
        @pl.when((kind >= 12) & (kind < 14))
        def _():
            o_ref[...] = _silu(_conv3(acc, cw_ref, seq_len)).astype(BF16)

    if reachable(14, 15):
        @pl.when(kind == 14)
        def _():
            for hh in range(ATTN_KV_HEADS):
                a = acc[:, hh * LANES:(hh + 1) * LANES]
                o_ref[:, hh * LANES:(hh + 1) * LANES] = qk_norm_rope(a, kn_ref[...], 1.0).astype(BF16)
            o_ref[:, ATTN_KV_HEADS * LANES:] = acc[:, ATTN_KV_HEADS * LANES:].astype(BF16)


def _input_projection(x2, shift, scale, w_main, w_small, qn, kn, cw, cos, sin, *, tm, seq_len, tile0, ntiles, rope):
    m, d = x2.shape
    nrow = m // tm
    per_row_mod = shift.shape[0] > 1
    mod_map = (lambda i, j: (i, 0, 0)) if per_row_mod else (lambda i, j: (0, 0, 0))
    conv_tile0 = COL_GQ // PROJ_TILE
    n_conv_tiles = cw.shape[1] // PROJ_TILE
    kern = functools.partial(_inproj_kernel, tile0=tile0, ntiles=ntiles, seq_len=seq_len, rope=rope)
    return pl.pallas_call(
        kern,
        grid=(nrow, ntiles),
        in_specs=[
            pl.BlockSpec((tm, d), lambda i, j: (i, 0)),
            pl.BlockSpec((1, 1, d), mod_map),
            pl.BlockSpec((1, 1, d), mod_map),
            pl.BlockSpec((d, PROJ_TILE), lambda i, j: (0, j + tile0)),
            pl.BlockSpec((d, LANES), lambda i, j: (0, 0)),
            pl.BlockSpec((1, LANES), lambda i, j: (0, 0)),
            pl.BlockSpec((1, LANES), lambda i, j: (0, 0)),
            pl.BlockSpec((3, PROJ_TILE), lambda i, j: (0, jnp.clip(j + tile0 - conv_tile0, 0, n_conv_tiles - 1))),
            pl.BlockSpec(cos.shape, lambda i, j: (0, 0)),
            pl.BlockSpec(sin.shape, lambda i, j: (0, 0)),
        ],
        out_specs=[pl.BlockSpec((tm, PROJ_TILE), lambda i, j: (i, j)),
                   pl.BlockSpec((tm, LANES), lambda i, j: (i, 0))],
        out_shape=[jax.ShapeDtypeStruct((m, ntiles * PROJ_TILE), BF16),
                   jax.ShapeDtypeStruct((m, LANES), F32)],
        scratch_shapes=[pltpu.VMEM((tm, d), BF16)],
        compiler_params=pltpu.CompilerParams(dimension_semantics=("arbitrary", "arbitrary"),
                                             vmem_limit_bytes=VMEM_LIMIT),
    )(x2, shift, scale, w_main, w_small, qn, kn, cw, cos, sin)


def _gate_kernel(g_ref, alog_ref, dtb_ref, o_ref, *, nrows):
    db = g_ref[0, :nrows, :]
    da = g_ref[0, nrows:, :]
    beta = _sigmoid(db)
    z = da + dtb_ref[...]
    softplus = jnp.maximum(z, 0.0) + jnp.log(1.0 + jnp.exp(-jnp.abs(z)))
    la = -jnp.exp(alog_ref[...]) * softplus
    half = nrows // 2
    lane = lax.broadcasted_iota(jnp.int32, (half, LANES), 1)
    pre = la[:half]
    suf = la[half:]
    k = 1
    while k < LANES:
        pre = pre + jnp.where(lane >= k, pltpu.roll(pre, k, axis=1), 0.0)
        suf = suf + jnp.where(lane < LANES - k, pltpu.roll(suf, LANES - k, axis=1), 0.0)
        k *= 2
    gam = jnp.concatenate([pre, suf], axis=0)
    tot = jnp.sum(la, axis=-1, keepdims=True)
    eg = jnp.exp(gam)
    o_ref[0, 0] = beta
    o_ref[0, 1] = gam
    o_ref[0, 2] = eg
    o_ref[0, 3] = beta * eg
    o_ref[0, 4] = jnp.exp(tot - gam)
    o_ref[0, 5] = jnp.broadcast_to(jnp.exp(tot), gam.shape)


def _gates(g, alog_rows, dtb_rows):
    b, r2, _ = g.shape
    nrows = r2 // 2
    return pl.pallas_call(
        functools.partial(_gate_kernel, nrows=nrows),
        grid=(b,),
        in_specs=[pl.BlockSpec((1, r2, LANES), lambda i: (i, 0, 0)),
                  pl.BlockSpec((nrows, 1), lambda i: (0, 0)),
                  pl.BlockSpec((nrows, 1), lambda i: (0, 0))],
        out_specs=pl.BlockSpec((1, 6, nrows, LANES), lambda i: (i, 0, 0, 0)),
        out_shape=jax.ShapeDtypeStruct((b, 6, nrows, LANES), F32),
    )(g, alog_rows, dtb_rows)


def _attn_kernel(q_ref, kc_ref, vc_ref, kx_ref, vx_ref, o_ref):
    kc = kc_ref[...]
    kx = kx_ref[...]
    vc = vc_ref[...]
    vx = vx_ref[...]
    for hh in range(ATTN_GROUP):
        q = q_ref[:, hh * LANES:(hh + 1) * LANES]
        sc = _dot_nt(q, kc)
        sx = _dot_nt(q, kx)
        m = jnp.maximum(jnp.max(sc, axis=-1, keepdims=True), jnp.max(sx, axis=-1, keepdims=True))
        pc = jnp.exp(sc - m)
        px = jnp.exp(sx - m)
        denom = jnp.sum(pc, axis=-1, keepdims=True) + jnp.sum(px, axis=-1, keepdims=True)
        o = _dot(pc.astype(BF16), vc) + _dot(px.astype(BF16), vx)
        o_ref[:, hh * LANES:(hh + 1) * LANES] = (o / denom).astype(BF16)


def _attention(p_lat, p_ctx, *, b, n, cl, tq):
    gw = ATTN_GROUP * LANES
    nq = n // tq
    q0 = COL_AQ // gw
    kx0, vx0 = COL_AK // LANES, COL_AV // LANES
    kc0, vc0 = (COL_AK - CTX_COL0) // LANES, (COL_AV - CTX_COL0) // LANES
    return pl.pallas_call(
        _attn_kernel,
        grid=(b, ATTN_KV_HEADS, nq),
        in_specs=[
            pl.BlockSpec((tq, gw), lambda i, h, t: (i * nq + t, q0 + h)),
            pl.BlockSpec((cl, LANES), lambda i, h, t: (i, kc0 + h)),
            pl.BlockSpec((cl, LANES), lambda i, h, t: (i, vc0 + h)),
            pl.BlockSpec((n, LANES), lambda i, h, t: (i, kx0 + h)),
            pl.BlockSpec((n, LANES), lambda i, h, t: (i, vx0 + h)),
        ],
        out_specs=pl.BlockSpec((tq, gw), lambda i, h, t: (i * nq + t, h)),
        out_shape=jax.ShapeDtypeStruct((b * n, ATTN_HEADS * LANES), BF16),
        compiler_params=pltpu.CompilerParams(dimension_semantics=("arbitrary", "arbitrary", "arbitrary"),
                                             vmem_limit_bytes=VMEM_LIMIT),
    )(p_lat, p_ctx, p_ctx, p_lat, p_lat)


def _gdn_kernel(ql_ref, kl_ref, vl_ref, kc_ref, vc_ref, z_ref, row_ref, col_ref, nw_ref, o_ref,
                u_ref, wq_ref, kdt_ref, p_ref, s_ref, oacc_ref, *, hb, ncc, ncl):
    c_sz = GDN_CHUNK
    nc = ncc + ncl
    ri = lax.broadcasted_iota(jnp.int32, (c_sz, c_sz), 0)
    ci = lax.broadcasted_iota(jnp.int32, (c_sz, c_sz), 1)
    eye = (ri == ci).astype(F32)
    levels = int(math.log2(c_sz))

    def tri_masks(d):
        lo, hi = (ci, ri) if d == 0 else (ri, ci)
        incl = hi >= lo
        strict = hi > lo
        pair = []
        for lv in range(levels):
            bh = hi >> lv
            bl = lo >> lv
            pair.append(((bh & 1) == 1) & (bl == bh - 1))
        return incl, strict, pair

    masks = [tri_masks(0), tri_masks(1)]

    def row_q(qi, d, hh, c):
        base = ((qi * 2 + d) * hb + hh) * nc
        return row_ref[0, 0, pl.ds(base + c, 1), :]

    def col_q(qi, d, hh, r0):
        lane = (qi * 2 + d) * hb + hh
        return col_ref[0, 0, pl.ds(r0, c_sz), lane:lane + 1]

    def prep(c, k, v, q):
        r0 = pl.multiple_of(c * c_sz, c_sz)
        rq = pl.multiple_of(c * (2 * c_sz), 2 * c_sz)
        for hh in range(hb):
            kh = k[hh]
            vh = v[hh].astype(F32)
            kf = kh.astype(F32)
            kk = _dot_nt(kh, kh)
            kt = kf.T
            if q is not None:
                qk = _dot_nt(q[hh], kh)
                qf = q[hh].astype(F32)
            for d in range(2):
                hd = hh * 2 + d
                incl, strict, pair = masks[d]
                beta_c = col_q(0, d, hh, r0)
                gam_c = col_q(1, d, hh, r0)
                eg_c = col_q(2, d, hh, r0)
                beg_c = col_q(3, d, hh, r0)
                gam_r = row_q(0, d, hh, c)
                egd_r = row_q(1, d, hh, c)
                e = jnp.exp(jnp.where(incl, gam_c - gam_r, NEG_BIG))
                lm = jnp.where(strict, beta_c * e * kk, 0.0)
                t = eye - jnp.where(pair[0], lm, 0.0)
                for lv in range(1, levels):
                    ms = jnp.where(pair[lv], lm, 0.0).astype(BF16)
                    xm = _dot(ms, t.astype(BF16))
                    t = t - _dot(t.astype(BF16), xm.astype(BF16))
                rhs = jnp.concatenate([beta_c * vh, beg_c * kf], axis=1).astype(BF16)
                uw = _dot(t.astype(BF16), rhs)
                u_ref[hd, pl.ds(r0, c_sz), :] = uw[:, :GDN_DV]
                wq_ref[hd, pl.ds(rq, c_sz), :] = uw[:, GDN_DV:].astype(BF16)
                kdt_ref[hd, pl.ds(r0, c_sz), :] = (kt * egd_r).astype(BF16)
                if q is not None:
                    wq_ref[hd, pl.ds(rq + c_sz, c_sz), :] = (qf * eg_c).astype(BF16)
                    p_ref[hd, pl.ds(r0, c_sz), :] = (e * qk).astype(BF16)

    def split_heads(ref, r0):
        return [ref[pl.ds(r0, c_sz), hh * LANES:(hh + 1) * LANES] for hh in range(hb)]

    def prep_ctx(c, carry):
        r0 = pl.multiple_of(c * c_sz, c_sz)
        prep(c, split_heads(kc_ref, r0), split_heads(vc_ref, r0), None)
        return carry

    def prep_lat(c, carry):
        r0 = pl.multiple_of(c * c_sz, c_sz)
        prep(c + ncc, split_heads(kl_ref, r0), split_heads(vl_ref, r0), split_heads(ql_ref, r0))
        return carry

    lax.fori_loop(0, ncc, prep_ctx, 0)
    lax.fori_loop(0, ncl, prep_lat, 0)

    s_ref[...] = jnp.zeros_like(s_ref)

    def scan_step(hh, d, c, out_row):
        hd = hh * 2 + d
        r0 = pl.multiple_of(c * c_sz, c_sz)
        rq = pl.multiple_of(c * (2 * c_sz), 2 * c_sz)
        s = s_ref[hd]
        sb = s.astype(BF16)
        gl = row_q(2, d, hh, c)
        if out_row is None:
            ws = _dot(wq_ref[hd, pl.ds(rq, c_sz), :], sb)
        else:
            wqs = _dot(wq_ref[hd, pl.ds(rq, 2 * c_sz), :], sb)
            ws = wqs[:c_sz]
        delta = (u_ref[hd, pl.ds(r0, c_sz), :] - ws).astype(BF16)
        s_ref[hd] = gl * s + _dot(kdt_ref[hd, pl.ds(r0, c_sz), :], delta)
        if out_row is not None:
            oacc_ref[hd, pl.ds(out_row, c_sz), :] = wqs[c_sz:] + _dot(p_ref[hd, pl.ds(r0, c_sz), :], delta)

    def scan_ctx(i, carry):
        for hh in range(hb):
            scan_step(hh, 0, i, None)
            scan_step(hh, 1, ncc - 1 - i, None)
        return carry

    def scan_lat(i, carry):
        for hh in range(hb):
            scan_step(hh, 0, ncc + i, pl.multiple_of(i * c_sz, c_sz))
            scan_step(hh, 1, ncc + ncl - 1 - i, pl.multiple_of((ncl - 1 - i) * c_sz, c_sz))
        return carry

    lax.fori_loop(0, ncc, scan_ctx, 0)
    lax.fori_loop(0, ncl, scan_lat, 0)

    def finish(i, carry):
        r0 = pl.multiple_of(i * c_sz, c_sz)
        for hh in range(hb):
            o = oacc_ref[2 * hh, pl.ds(r0, c_sz), :] + oacc_ref[2 * hh + 1, pl.ds(r0, c_sz), :]
            o = o * lax.rsqrt(jnp.mean(o * o, axis=-1, keepdims=True) + NORM_EPS) * nw_ref[...]
            zf = z_ref[pl.ds(r0, c_sz), hh * LANES:(hh + 1) * LANES].astype(F32)
            o_ref[pl.ds(r0, c_sz), hh * LANES:(hh + 1) * LANES] = (o * _silu(zf)).astype(BF16)
        return carry

    lax.fori_loop(0, ncl, finish, 0)


def _gdn(p_lat, p_ctx, rowpack, colpack, nw, *, b, n, cl, hb):
    gw = hb * LANES
    ng = GDN_HEADS // hb
    ncc, ncl = cl // GDN_CHUNK, n // GDN_CHUNK
    nc = ncc + ncl
    nt = nc * GDN_CHUNK
    q0, k0, v0, z0 = COL_GQ // gw, COL_GK // gw, COL_GV // gw, COL_Z // gw
    kc0, vc0 = (COL_GK - CTX_COL0) // gw, (COL_GV - CTX_COL0) // gw
    kern = functools.partial(_gdn_kernel, hb=hb, ncc=ncc, ncl=ncl)
    return pl.pallas_call(
        kern,
        grid=(b, ng),
        in_specs=[
            pl.BlockSpec((n, gw), lambda i, g: (i, q0 + g)),
            pl.BlockSpec((n, gw), lambda i, g: (i, k0 + g)),
            pl.BlockSpec((n, gw), lambda i, g: (i, v0 + g)),
            pl.BlockSpec((cl, gw), lambda i, g: (i, kc0 + g)),
            pl.BlockSpec((cl, gw), lambda i, g: (i, vc0 + g)),
            pl.BlockSpec((n, gw), lambda i, g: (i, z0 + g)),
            pl.BlockSpec((1, 1) + rowpack.shape[2:], lambda i, g: (i, g, 0, 0)),
            pl.BlockSpec((1, 1) + colpack.shape[2:], lambda i, g: (i, g, 0, 0)),
            pl.BlockSpec((1, LANES), lambda i, g: (0, 0)),
        ],
        out_specs=pl.BlockSpec((n, gw), lambda i, g: (i, g)),
        out_shape=jax.ShapeDtypeStruct((b * n, GDN_HEADS * LANES), BF16),
        scratch_shapes=[
            pltpu.VMEM((2 * hb, nt, LANES), F32),
            pltpu.VMEM((2 * hb, 2 * nt, LANES), BF16),
            pltpu.VMEM((2 * hb, nt, LANES), BF16),
            pltpu.VMEM((2 * hb, nt, LANES), BF16),
            pltpu.VMEM((2 * hb, GDN_DK, GDN_DV), F32),
            pltpu.VMEM((2 * hb, n, LANES), F32),
        ],
        compiler_params=pltpu.CompilerParams(dimension_semantics=("arbitrary", "arbitrary"),
                                             vmem_limit_bytes=VMEM_LIMIT),
    )(p_lat, p_lat, p_lat, p_ctx, p_ctx, p_lat, rowpack, colpack, nw)


def _merge_kernel(x_ref, a_ref, g_ref, ga_ref, gd_ref, g1_ref, wpa_ref, wpd_ref, wout_ref, o_ref):
    ya = _dot(a_ref[...], wpa_ref[...])
    yd = _dot(g_ref[...], wpd_ref[...])
    y = ga_ref[...].astype(F32) * ya + gd_ref[...].astype(F32) * yd
    o_ref[...] = x_ref[...] + g1_ref[0] * _dot(y.astype(BF16), wout_ref[...])


def _merge(x2, attn, gdn, p_lat, g1, wpa, wpd, wout, *, n, tm):
    m, d = x2.shape
    per_seq = n // tm
    return pl.pallas_call(
        _merge_kernel,
        grid=(m // tm,),
        in_specs=[
            pl.BlockSpec((tm, d), lambda i: (i, 0)),
            pl.BlockSpec((tm, d), lambda i: (i, 0)),
            pl.BlockSpec((tm, d), lambda i: (i, 0)),
            pl.BlockSpec((tm, d), lambda i: (i, COL_GATES // d)),
            pl.BlockSpec((tm, d), lambda i: (i, COL_GATES // d + 1)),
            pl.BlockSpec((1, 1, d), lambda i: (i // per_seq, 0, 0)),
            pl.BlockSpec((d, d), lambda i: (0, 0)),
            pl.BlockSpec((d, d), lambda i: (0, 0)),
            pl.BlockSpec((d, d), lambda i: (0, 0)),
        ],
        out_specs=pl.BlockSpec((tm, d), lambda i: (i, 0)),
        out_shape=jax.ShapeDtypeStruct((m, d), F32),
        compiler_params=pltpu.CompilerParams(dimension_semantics=("arbitrary",), vmem_limit_bytes=VMEM_LIMIT),
    )(x2, attn, gdn, p_lat, p_lat, g1, wpa, wpd, wout)


def _ffn_kernel(x_ref, sh_ref, sc_ref, g2_ref, wg_ref, wv_ref, cwg_ref, cwv_ref, cbg_ref, cbv_ref, wd_ref, fw_ref,
                o_ref, h_ref, *, seq_len):
    j = pl.program_id(1)

    @pl.when(j == 0)
    def _():
        x = x_ref[...]
        r = lax.rsqrt(jnp.mean(x * x, axis=-1, keepdims=True) + NORM_EPS)
        h_ref[...] = ((x * r) * (1.0 + sc_ref[0]) + sh_ref[0]).astype(BF16)

    h = h_ref[...]
    ug = _conv3(_dot(h, wg_ref[...]), cwg_ref, seq_len) + cbg_ref[...]
    uv = _conv3(_dot(h, wv_ref[...]), cwv_ref, seq_len) + cbv_ref[...]
    contrib = _dot((_silu(ug) * uv).astype(BF16), wd_ref[...])

    @pl.when(j == 0)
    def _():
        o_ref[...] = contrib

    @pl.when(j > 0)
    def _():
        o_ref[...] += contrib

    @pl.when(j == pl.num_programs(1) - 1)
    def _():
        y = x_ref[...] + g2_ref[0] * o_ref[...]
        o_ref[...] = y * lax.rsqrt(jnp.mean(y * y, axis=-1, keepdims=True) + NORM_EPS) * fw_ref[...]


def _ffn(x1, sh2, sc2, g2, wup, cw, cb, wdown, fw, *, n, tf):
    m, d = x1.shape
    dff = wdown.shape[0]
    nj = dff // tf
    mod_map = lambda i, j: (i, 0, 0)
    return pl.pallas_call(
        functools.partial(_ffn_kernel, seq_len=n),
        grid=(m // n, nj),
        in_specs=[
            pl.BlockSpec((n, d), lambda i, j: (i, 0)),
            pl.BlockSpec((1, 1, d), mod_map),
            pl.BlockSpec((1, 1, d), mod_map),
            pl.BlockSpec((1, 1, d), mod_map),
            pl.BlockSpec((d, tf), lambda i, j: (0, j)),
            pl.BlockSpec((d, tf), lambda i, j: (0, j + nj)),
            pl.BlockSpec((3, tf), lambda i, j: (0, j)),
            pl.BlockSpec((3, tf), lambda i, j: (0, j + nj)),
            pl.BlockSpec((1, tf), lambda i, j: (0, j)),
            pl.BlockSpec((1, tf), lambda i, j: (0, j + nj)),
            pl.BlockSpec((tf, d), lambda i, j: (j, 0)),
            pl.BlockSpec((1, d), lambda i, j: (0, 0)),
        ],
        out_specs=pl.BlockSpec((n, d), lambda i, j: (i, 0)),
        out_shape=jax.ShapeDtypeStruct((m, d), F32),
        scratch_shapes=[pltpu.VMEM((n, d), BF16)],
        compiler_params=pltpu.CompilerParams(dimension_semantics=("arbitrary", "arbitrary"),
                                             vmem_limit_bytes=VMEM_LIMIT),
    )(x1, sh2, sc2, g2, wup, wup, cw, cw, cb, cb, wdown, fw)


def _rope_tables(n):
    rows = n // GRID_W
    row_ids = jnp.broadcast_to(jnp.arange(rows, dtype=F32)[:, None], (rows, GRID_W)).reshape(n)
    col_ids = jnp.broadcast_to(jnp.arange(GRID_W, dtype=F32)[None, :], (rows, GRID_W)).reshape(n)
    inv_freq = ROPE_THETA ** (-jnp.arange(0, ROPE_AXIS_DIM, 2, dtype=F32) / ROPE_AXIS_DIM)
    ar = row_ids[:, None] * inv_freq
    ac = col_ids[:, None] * inv_freq
    cos = jnp.concatenate([jnp.cos(ar), jnp.cos(ar), jnp.cos(ac), jnp.cos(ac)], axis=-1)
    sin = jnp.concatenate([-jnp.sin(ar), jnp.sin(ar), -jnp.sin(ac), jnp.sin(ac)], axis=-1)
    return cos, sin


def kernel(x, c, ctx, c_ctx, w_mod, b_mod, w_in, q_norm_w, k_norm_w, conv_qkv_w, a_log, dt_bias, gdn_norm_w, w_pa,
           w_pd, w_out, w_up, ffn_conv_w, ffn_conv_b, w_down, final_norm_w):
    assert w_mod.shape[0] == 1, "single-layer block"
    b, n, d = x.shape
    cl = ctx.shape[1]
    hb = 2

    pad = (-(b + 1)) % 8
    cc = jnp.concatenate([c, c_ctx[None, :], jnp.zeros((pad, d), F32)], axis=0)
    mod = _modulation(cc, w_mod[0], b_mod[0][None, :])
    sh1, sc1, g1, sh2, sc2, g2 = [t[:, None, :] for t in jnp.split(mod[:b], 6, axis=-1)]
    csh1, csc1 = mod[b:b + 1, None, :d], mod[b:b + 1, None, d:2 * d]

    w = w_in[0]
    akv, qkv_w = ATTN_KV_HEADS * HEAD_DIM, 3 * GDN_HEADS * GDN_DK
    o_qkv = 2 * akv
    o_db = o_qkv + qkv_w
    o_aq = o_db + 4 * GDN_HEADS
    o_z = o_aq + ATTN_HEADS * HEAD_DIM
    o_gate = o_z + GDN_HEADS * GDN_DV
    w_main = jnp.concatenate([w[:, o_gate:], w[:, o_aq:o_z], w[:, o_z:o_gate], w[:, o_qkv:o_db], w[:, :o_qkv]],
                             axis=1).astype(BF16)
    w_small = jnp.concatenate([w[:, o_db:o_aq], jnp.zeros((d, LANES - 4 * GDN_HEADS), F32)], axis=1).astype(BF16)
    cos, sin = _rope_tables(n)
    x2 = x.reshape(b * n, d)
    common = (w_main, w_small, q_norm_w, k_norm_w, conv_qkv_w[0], cos, sin)
    p_lat, s_lat = _input_projection(x2, sh1, sc1, *common, tm=n, seq_len=n, tile0=0,
                                     ntiles=MAIN_COLS // PROJ_TILE, rope=True)
    p_ctx, s_ctx = _input_projection(ctx.reshape(b * cl, d), csh1, csc1, *common, tm=b * cl, seq_len=cl,
                                     tile0=CTX_COL0 // PROJ_TILE, ntiles=(MAIN_COLS - CTX_COL0) // PROJ_TILE,
                                     rope=False)

    ncc, ncl = cl // GDN_CHUNK, n // GDN_CHUNK
    nc = ncc + ncl
    ng = GDN_HEADS // hb
    nsm = 4 * GDN_HEADS
    small = jnp.concatenate([s_ctx[:, :nsm].reshape(b, cl, nsm), s_lat[:, :nsm].reshape(b, n, nsm)], axis=1)
    small = small.transpose(0, 2, 1).reshape(b, nsm * nc, GDN_CHUNK)
    alog_rows = jnp.repeat(a_log[0].reshape(-1), nc)[:, None]
    dtb_rows = jnp.repeat(dt_bias[0].reshape(-1), nc)[:, None]
    gq = _gates(small, alog_rows, dtb_rows).reshape(b, 6, 2, ng, hb, nc, GDN_CHUNK)
    rowpack = gq[:, jnp.array([1, 4, 5])].transpose(0, 3, 1, 2, 4, 5, 6).reshape(b, ng, 3 * 2 * hb * nc, GDN_CHUNK)
    colpack = gq[:, :4].transpose(0, 3, 5, 6, 1, 2, 4).reshape(b, ng, nc * GDN_CHUNK, 4 * 2 * hb)

    attn = _attention(p_lat, p_ctx, b=b, n=n, cl=cl, tq=min(n, 512))
    gdn = _gdn(p_lat, p_ctx, rowpack, colpack, gdn_norm_w, b=b, n=n, cl=cl, hb=hb)

    x1 = _merge(x2, attn, gdn, p_lat, g1, w_pa[0].astype(BF16), w_pd[0].astype(BF16), w_out[0].astype(BF16),
                n=n, tm=min(n, 1024))

    out = _ffn(x1, sh2, sc2, g2, w_up[0].astype(BF16), ffn_conv_w[0], ffn_conv_b[0][None, :],
               w_down[0].astype(BF16), final_norm_w[None, :], n=n, tf=256)
    return out.reshape(b, n, d)
```

```python
import functools
import math

import jax
import jax.numpy as jnp
from jax import lax
from jax.experimental import pallas as pl
from jax.experimental.pallas import tpu as pltpu

F32 = jnp.float32
BF16 = jnp.bfloat16

LANES = 128
GRID_W = 64
ATTN_HEADS = 8
ATTN_KV_HEADS = 2
ATTN_GROUP = ATTN_HEADS // ATTN_KV_HEADS
HEAD_DIM = 128
ROPE_AXIS_DIM = HEAD_DIM // 2
ROPE_THETA = 10000.0
GDN_HEADS = 8
GDN_DK = 128
GDN_DV = 128
GDN_CHUNK = 128
NORM_EPS = 1e-6
NEG_BIG = -1e30

PROJ_TILE = 512
COL_GATES = 0
COL_AQ = 2048
COL_Z = 3072
COL_GQ = 4096
COL_GK = 5120
COL_GV = 6144
COL_AK = 7168
COL_AV = 7424
MAIN_COLS = 7680
CTX_COL0 = COL_GQ
VMEM_LIMIT = 56 * 1024 * 1024


def _dot(a, b):
    return jnp.dot(a, b, preferred_element_type=F32)


def _dot_nt(a, b):
    return lax.dot_general(a, b, (((1,), (1,)), ((), ())), preferred_element_type=F32)


def _sigmoid(x):
    return 1.0 / (1.0 + jnp.exp(-x))


def _silu(x):
    return x * _sigmoid(x)


def _mod_kernel(c_ref, w_ref, b_ref, o_ref):
    s = _silu(c_ref[...])
    w = w_ref[...]
    s_hi = s.astype(BF16)
    s_lo = (s - s_hi.astype(F32)).astype(BF16)
    w_hi = w.astype(BF16)
    w_lo = (w - w_hi.astype(F32)).astype(BF16)
    o_ref[...] = _dot(s_hi, w_hi) + _dot(s_lo, w_hi) + _dot(s_hi, w_lo) + b_ref[...]


def _modulation(cc, w_mod, b_mod):
    rows, d = cc.shape
    n = w_mod.shape[1]
    tn = 1024
    return pl.pallas_call(
        _mod_kernel,
        grid=(n // tn,),
        in_specs=[pl.BlockSpec((rows, d), lambda j: (0, 0)),
                  pl.BlockSpec((d, tn), lambda j: (0, j)),
                  pl.BlockSpec((1, tn), lambda j: (0, j))],
        out_specs=pl.BlockSpec((rows, tn), lambda j: (0, j)),
        out_shape=jax.ShapeDtypeStruct((rows, n), F32),
    )(cc, w_mod, b_mod)


ROW_BLOCK = 256
ROW_HALO = 16


def _skewed(n, produce, consume):
    prev = produce(0)
    for r in range(1, n):
        cur = produce(r)
        consume(r - 1, prev)
        prev = cur
    consume(n - 1, prev)


def _block_rows(r, tm, seq_len):
    r0 = r * ROW_BLOCK
    halo = ROW_HALO if seq_len > ROW_BLOCK else 0
    return r0, max(r0 - halo, 0), min(r0 + ROW_BLOCK + halo, tm)


def _conv3_block(ext, w_ref, r0, lo, seq_len):
    assert seq_len % ROW_BLOCK == 0
    n_ext = ext.shape[0]
    off = r0 - lo
    prev = pltpu.roll(ext, 1, axis=0)[off:off + ROW_BLOCK]
    nxt = pltpu.roll(ext, n_ext - 1, axis=0)[off:off + ROW_BLOCK]
    row = lax.broadcasted_iota(jnp.int32, (ROW_BLOCK, 1), 0)
    if r0 % seq_len == 0:
        prev = jnp.where(row == 0, 0.0, prev)
    if (r0 + ROW_BLOCK) % seq_len == 0:
        nxt = jnp.where(row == ROW_BLOCK - 1, 0.0, nxt)
    return prev * w_ref[0:1, :] + ext[off:off + ROW_BLOCK] * w_ref[1:2, :] + nxt * w_ref[2:3, :]


def _rope(a, cos, sin):
    lane = lax.broadcasted_iota(jnp.int32, a.shape, 1)
    swapped = jnp.where((lane & 32) == 0, pltpu.roll(a, 96, axis=1), pltpu.roll(a, 32, axis=1))
    return a * cos + swapped * sin


def _inproj_kernel(x_ref, sh_ref, sc_ref, w_ref, ws_ref, qn_ref, kn_ref, cw_ref, cos_ref, sin_ref,
                   o_ref, os_ref, hx_ref, *, tile0, ntiles, seq_len, rope):
    j = pl.program_id(1)
    tm = hx_ref.shape[0]
    assert tm % ROW_BLOCK == 0

    @pl.when(j == 0)
    def _():
        x = x_ref[...]
        r = lax.rsqrt(jnp.mean(x * x, axis=-1, keepdims=True) + NORM_EPS)
        h = (x * r) * (1.0 + sc_ref[0]) + sh_ref[0]
        hb = h.astype(BF16)
        hx_ref[...] = hb
        os_ref[...] = _dot(hb, ws_ref[...])

    kind = j + tile0
    nh = PROJ_TILE // LANES

    def head(a, hh):
        return a[:, hh * LANES:(hh + 1) * LANES]

    def store_head(r0, hh, a):
        o_ref[r0:r0 + ROW_BLOCK, hh * LANES:(hh + 1) * LANES] = a.astype(BF16)

    def qk_norm_rope(a, w, scale, r0):
        a = a * lax.rsqrt(jnp.mean(a * a, axis=-1, keepdims=True) + NORM_EPS) * w
        if rope:
            a = _rope(a, cos_ref[r0:r0 + ROW_BLOCK, :], sin_ref[r0:r0 + ROW_BLOCK, :])
        return a * scale if scale != 1.0 else a

    def ep_gates(acc, r0, lo):
        o_ref[r0:r0 + ROW_BLOCK, :] = _sigmoid(acc).astype(BF16)

    def ep_attn_q(acc, r0, lo):
        for hh in range(nh):
            store_head(r0, hh, qk_norm_rope(head(acc, hh), qn_ref[...], HEAD_DIM ** -0.5, r0))

    def ep_plain(acc, r0, lo):
        o_ref[r0:r0 + ROW_BLOCK, :] = acc.astype(BF16)

    def ep_gdn_qk(scale):
        def ep(ext, r0, lo):
            y = _silu(_conv3_block(ext, cw_ref, r0, lo, seq_len))
            for hh in range(nh):
                a = head(y, hh)
                store_head(r0, hh, a * (lax.rsqrt(jnp.sum(a * a, axis=-1, keepdims=True) + NORM_EPS) * scale))
        return ep

    def ep_gdn_v(ext, r0, lo):
        o_ref[r0:r0 + ROW_BLOCK, :] = _silu(_conv3_block(ext, cw_ref, r0, lo, seq_len)).astype(BF16)

    def ep_attn_kv(acc, r0, lo):
        for hh in range(ATTN_KV_HEADS):
            store_head(r0, hh, qk_norm_rope(head(acc, hh), kn_ref[...], 1.0, r0))
        o_ref[r0:r0 + ROW_BLOCK, ATTN_KV_HEADS * LANES:] = acc[:, ATTN_KV_HEADS * LANES:].astype(BF16)

    groups = [(0, 4, ep_gates, False), (4, 6, ep_attn_q, False), (6, 8, ep_plain, False),
              (8, 10, ep_gdn_qk(GDN_DK ** -0.5), True), (10, 12, ep_gdn_qk(1.0), True), (12, 14, ep_gdn_v, True),
              (14, 15, ep_attn_kv, False)]

    for k_lo, k_hi, epilogue, conv in groups:
        if max(k_lo, tile0) >= min(k_hi, tile0 + ntiles):
            continue

        def run(epilogue=epilogue, conv=conv):
            def rows(r):
                return _block_rows(r, tm, seq_len) if conv else (r * ROW_BLOCK, r * ROW_BLOCK, (r + 1) * ROW_BLOCK)

            def produce(r):
                _, lo, hi = rows(r)
                return _dot(hx_ref[lo:hi, :], w_ref[...])

            def consume(r, acc):
                r0, lo, _ = rows(r)
                epilogue(acc, r0, lo)

            _skewed(tm // ROW_BLOCK, produce, consume)

        pl.when((kind >= k_lo) & (kind < k_hi))(run)


def _input_projection(x2, shift, scale, w_main, w_small, qn, kn, cw, cos, sin, *, tm, seq_len, tile0, ntiles, rope):
    m, d = x2.shape
    nrow = m // tm
    per_row_mod = shift.shape[0] > 1
    mod_map = (lambda i, j: (i, 0, 0)) if per_row_mod else (lambda i, j: (0, 0, 0))
    conv_tile0 = COL_GQ // PROJ_TILE
    n_conv_tiles = cw.shape[1] // PROJ_TILE
    kern = functools.partial(_inproj_kernel, tile0=tile0, ntiles=ntiles, seq_len=seq_len, rope=rope)
    return pl.pallas_call(
        kern,
        grid=(nrow, ntiles),
        in_specs=[
            pl.BlockSpec((tm, d), lambda i, j: (i, 0)),
            pl.BlockSpec((1, 1, d), mod_map),
            pl.BlockSpec((1, 1, d), mod_map),
            pl.BlockSpec((d, PROJ_TILE), lambda i, j: (0, j + tile0)),
            pl.BlockSpec((d, LANES), lambda i, j: (0, 0)),
            pl.BlockSpec((1, LANES), lambda i, j: (0, 0)),
            pl.BlockSpec((1, LANES), lambda i, j: (0, 0)),
            pl.BlockSpec((3, PROJ_TILE), lambda i, j: (0, jnp.clip(j + tile0 - conv_tile0, 0, n_conv_tiles - 1))),
            pl.BlockSpec(cos.shape, lambda i, j: (0, 0)),
            pl.BlockSpec(sin.shape, lambda i, j: (0, 0)),
        ],
        out_specs=[pl.BlockSpec((tm, PROJ_TILE), lambda i, j: (i, j)),
                   pl.BlockSpec((tm, LANES), lambda i, j: (i, 0))],
        out_shape=[jax.ShapeDtypeStruct((m, ntiles * PROJ_TILE), BF16),
                   jax.ShapeDtypeStruct((m, LANES), F32)],
        scratch_shapes=[pltpu.VMEM((tm, d), BF16)],
        compiler_params=pltpu.CompilerParams(dimension_semantics=("arbitrary", "arbitrary"),
                                             vmem_limit_bytes=VMEM_LIMIT),
    )(x2, shift, scale, w_main, w_small, qn, kn, cw, cos, sin)


def _gate_kernel(g_ref, alog_ref, dtb_ref, o_ref, *, nrows):
    db = g_ref[0, :nrows, :]
    da = g_ref[0, nrows:, :]
    beta = _sigmoid(db)
    z = da + dtb_ref[...]
    softplus = jnp.maximum(z, 0.0) + jnp.log(1.0 + jnp.exp(-jnp.abs(z)))
    la = -jnp.exp(alog_ref[...]) * softplus
    half = nrows // 2
    lane = lax.broadcasted_iota(jnp.int32, (half, LANES), 1)
    pre = la[:half]
    suf = la[half:]
    k = 1
    while k < LANES:
        pre = pre + jnp.where(lane >= k, pltpu.roll(pre, k, axis=1), 0.0)
        suf = suf + jnp.where(lane < LANES - k, pltpu.roll(suf, LANES - k, axis=1), 0.0)
        k *= 2
    gam = jnp.concatenate([pre, suf], axis=0)
    tot = jnp.sum(la, axis=-1, keepdims=True)
    eg = jnp.exp(gam)
    o_ref[0, 0] = beta
    o_ref[0, 1] = gam
    o_ref[0, 2] = eg
    o_ref[0, 3] = beta * eg
    o_ref[0, 4] = jnp.exp(tot - gam)
    o_ref[0, 5] = jnp.broadcast_to(jnp.exp(tot), gam.shape)


def _gates(g, alog_rows, dtb_rows):
    b, r2, _ = g.shape
    nrows = r2 // 2
    return pl.pallas_call(
        functools.partial(_gate_kernel, nrows=nrows),
        grid=(b,),
        in_specs=[pl.BlockSpec((1, r2, LANES), lambda i: (i, 0, 0)),
                  pl.BlockSpec((nrows, 1), lambda i: (0, 0)),
                  pl.BlockSpec((nrows, 1), lambda i: (0, 0))],
        out_specs=pl.BlockSpec((1, 6, nrows, LANES), lambda i: (i, 0, 0, 0)),
        out_shape=jax.ShapeDtypeStruct((b, 6, nrows, LANES), F32),
    )(g, alog_rows, dtb_rows)


def _attn_kernel(q_ref, kc_ref, vc_ref, kx_ref, vx_ref, o_ref):
    kc = kc_ref[...]
    kx = kx_ref[...]
    vc = vc_ref[...]
    vx = vx_ref[...]
    for hh in range(ATTN_GROUP):
        q = q_ref[:, hh * LANES:(hh + 1) * LANES]
        sc = _dot_nt(q, kc)
        sx = _dot_nt(q, kx)
        m = jnp.maximum(jnp.max(sc, axis=-1, keepdims=True), jnp.max(sx, axis=-1, keepdims=True))
        pc = jnp.exp(sc - m)
        px = jnp.exp(sx - m)
        denom = jnp.sum(pc, axis=-1, keepdims=True) + jnp.sum(px, axis=-1, keepdims=True)
        o = _dot(pc.astype(BF16), vc) + _dot(px.astype(BF16), vx)
        o_ref[:, hh * LANES:(hh + 1) * LANES] = (o / denom).astype(BF16)


def _attention(p_lat, p_ctx, *, b, n, cl, tq):
    gw = ATTN_GROUP * LANES
    nq = n // tq
    q0 = COL_AQ // gw
    kx0, vx0 = COL_AK // LANES, COL_AV // LANES
    kc0, vc0 = (COL_AK - CTX_COL0) // LANES, (COL_AV - CTX_COL0) // LANES
    return pl.pallas_call(
        _attn_kernel,
        grid=(b, ATTN_KV_HEADS, nq),
        in_specs=[
            pl.BlockSpec((tq, gw), lambda i, h, t: (i * nq + t, q0 + h)),
            pl.BlockSpec((cl, LANES), lambda i, h, t: (i, kc0 + h)),
            pl.BlockSpec((cl, LANES), lambda i, h, t: (i, vc0 + h)),
            pl.BlockSpec((n, LANES), lambda i, h, t: (i, kx0 + h)),
            pl.BlockSpec((n, LANES), lambda i, h, t: (i, vx0 + h)),
        ],
        out_specs=pl.BlockSpec((tq, gw), lambda i, h, t: (i * nq + t, h)),
        out_shape=jax.ShapeDtypeStruct((b * n, ATTN_HEADS * LANES), BF16),
        compiler_params=pltpu.CompilerParams(dimension_semantics=("arbitrary", "arbitrary", "arbitrary"),
                                             vmem_limit_bytes=VMEM_LIMIT),
    )(p_lat, p_ctx, p_ctx, p_lat, p_lat)


def _gdn_kernel(ql_ref, kl_ref, vl_ref, kc_ref, vc_ref, z_ref, row_ref, col_ref, nw_ref, o_ref,
                u_ref, wq_ref, kdt_ref, p_ref, s_ref, oacc_ref, *, hb, ncc, ncl):
    c_sz = GDN_CHUNK
    nc = ncc + ncl
    ri = lax.broadcasted_iota(jnp.int32, (c_sz, c_sz), 0)
    ci = lax.broadcasted_iota(jnp.int32, (c_sz, c_sz), 1)
    eye = (ri == ci).astype(F32)
    levels = int(math.log2(c_sz))
    cu_ctx = math.gcd(ncc, 2)
    cu_lat = math.gcd(ncl, 2)

    def tri_masks(d):
        lo, hi = (ci, ri) if d == 0 else (ri, ci)
        incl = hi >= lo
        strict = hi > lo
        pair = []
        for lv in range(levels):
            bh = hi >> lv
            bl = lo >> lv
            pair.append(((bh & 1) == 1) & (bl == bh - 1))
        return incl, strict, pair

    masks = [tri_masks(0), tri_masks(1)]

    def row_q(qi, d, hh, c):
        base = ((qi * 2 + d) * hb + hh) * nc
        return row_ref[0, 0, pl.ds(base + c, 1), :]

    def col_q(qi, d, hh, r0):
        lane = (qi * 2 + d) * hb + hh
        return col_ref[0, 0, pl.ds(r0, c_sz), lane:lane + 1]

    def prep(chunks):
        inst = []
        for c, k, v, q in chunks:
            r0 = pl.multiple_of(c * c_sz, c_sz)
            rq = pl.multiple_of(c * (2 * c_sz), 2 * c_sz)
            for hh in range(hb):
                kh = k[hh]
                kk = _dot_nt(kh, kh)
                qk = None if q is None else _dot_nt(q[hh], kh)
                for d in range(2):
                    inst.append(dict(c=c, r0=r0, rq=rq, hh=hh, d=d, hd=hh * 2 + d, kh=kh, vh=v[hh], kk=kk, qk=qk,
                                     qh=None if q is None else q[hh]))
        for it in inst:
            d, hh, r0 = it["d"], it["hh"], it["r0"]
            incl, strict, pair = masks[d]
            beta_c = col_q(0, d, hh, r0)
            e = jnp.exp(jnp.where(incl, col_q(1, d, hh, r0) - row_q(0, d, hh, it["c"]), NEG_BIG))
            lm = jnp.where(strict, beta_c * e * it["kk"], 0.0)
            it["t"] = eye - jnp.where(pair[0], lm, 0.0)
            it["ms"] = [jnp.where(pair[lv], lm, 0.0).astype(BF16) for lv in range(1, levels)]
            kf = it["kh"].astype(F32)
            it["rhs"] = jnp.concatenate([beta_c * it["vh"].astype(F32), col_q(3, d, hh, r0) * kf], axis=1).astype(BF16)
            kdt_ref[it["hd"], pl.ds(r0, c_sz), :] = (kf.T * row_q(1, d, hh, it["c"])).astype(BF16)
            if it["qk"] is not None:
                wq_ref[it["hd"], pl.ds(it["rq"] + c_sz, c_sz), :] = (
                    it["qh"].astype(F32) * col_q(2, d, hh, r0)).astype(BF16)
                p_ref[it["hd"], pl.ds(r0, c_sz), :] = (e * it["qk"]).astype(BF16)
        for lv in range(levels - 1):
            xs = [_dot(it["ms"][lv], it["t"].astype(BF16)).astype(BF16) for it in inst]
            for it, xm in zip(inst, xs):
                it["t"] = it["t"] - _dot(it["t"].astype(BF16), xm)
        uws = [_dot(it["t"].astype(BF16), it["rhs"]) for it in inst]
        for it, uw in zip(inst, uws):
            u_ref[it["hd"], pl.ds(it["r0"], c_sz), :] = uw[:, :GDN_DV]
            wq_ref[it["hd"], pl.ds(it["rq"], c_sz), :] = uw[:, GDN_DV:].astype(BF16)

    def split_heads(ref, r0):
        return [ref[pl.ds(r0, c_sz), hh * LANES:(hh + 1) * LANES] for hh in range(hb)]

    def prep_ctx(i, carry):
        chunks = []
        for j in range(cu_ctx):
            c = i * cu_ctx + j
            r0 = pl.multiple_of(c * c_sz, c_sz)
            chunks.append((c, split_heads(kc_ref, r0), split_heads(vc_ref, r0), None))
        prep(chunks)
        return carry

    def prep_lat(i, carry):
        chunks = []
        for j in range(cu_lat):
            c = i * cu_lat + j
            r0 = pl.multiple_of(c * c_sz, c_sz)
            chunks.append((c + ncc, split_heads(kl_ref, r0), split_heads(vl_ref, r0), split_heads(ql_ref, r0)))
        prep(chunks)
        return carry

    lax.fori_loop(0, ncc // cu_ctx, prep_ctx, 0)
    lax.fori_loop(0, ncl // cu_lat, prep_lat, 0)

    s_ref[...] = jnp.zeros_like(s_ref)

    def scan_steps(steps):
        st = []
        for hh, d, c, out_row in steps:
            hd = hh * 2 + d
            r0 = pl.multiple_of(c * c_sz, c_sz)
            rq = pl.multiple_of(c * (2 * c_sz), 2 * c_sz)
            s = s_ref[hd]
            rows = c_sz if out_row is None else 2 * c_sz
            st.append(dict(hd=hd, r0=r0, s=s, gl=row_q(2, d, hh, c), out_row=out_row,
                           wqs=_dot(wq_ref[hd, pl.ds(rq, rows), :], s.astype(BF16))))
        for it in st:
            it["delta"] = (u_ref[it["hd"], pl.ds(it["r0"], c_sz), :] - it["wqs"][:c_sz]).astype(BF16)
        for it in st:
            s_ref[it["hd"]] = it["gl"] * it["s"] + _dot(kdt_ref[it["hd"], pl.ds(it["r0"], c_sz), :], it["delta"])
        for it in st:
            if it["out_row"] is not None:
                oacc_ref[it["hd"], pl.ds(it["out_row"], c_sz), :] = (
                    it["wqs"][c_sz:] + _dot(p_ref[it["hd"], pl.ds(it["r0"], c_sz), :], it["delta"]))

    def scan_ctx(i, carry):
        steps = []
        for hh in range(hb):
            steps += [(hh, 0, i, None), (hh, 1, ncc - 1 - i, None)]
        scan_steps(steps)
        return carry

    def scan_lat(i, carry):
        steps = []
        for hh in range(hb):
            steps += [(hh, 0, ncc + i, pl.multiple_of(i * c_sz, c_sz)),
                      (hh, 1, ncc + ncl - 1 - i, pl.multiple_of((ncl - 1 - i) * c_sz, c_sz))]
        scan_steps(steps)
        return carry

    lax.fori_loop(0, ncc, scan_ctx, 0)
    lax.fori_loop(0, ncl, scan_lat, 0)

    def finish(i, carry):
        r0 = pl.multiple_of(i * c_sz, c_sz)
        for hh in range(hb):
            o = oacc_ref[2 * hh, pl.ds(r0, c_sz), :] + oacc_ref[2 * hh + 1, pl.ds(r0, c_sz), :]
            o = o * lax.rsqrt(jnp.mean(o * o, axis=-1, keepdims=True) + NORM_EPS) * nw_ref[...]
            zf = z_ref[pl.ds(r0, c_sz), hh * LANES:(hh + 1) * LANES].astype(F32)
            o_ref[pl.ds(r0, c_sz), hh * LANES:(hh + 1) * LANES] = (o * _silu(zf)).astype(BF16)
        return carry

    lax.fori_loop(0, ncl, finish, 0)


def _gdn(p_lat, p_ctx, rowpack, colpack, nw, *, b, n, cl, hb):
    gw = hb * LANES
    ng = GDN_HEADS // hb
    ncc, ncl = cl // GDN_CHUNK, n // GDN_CHUNK
    nc = ncc + ncl
    nt = nc * GDN_CHUNK
    q0, k0, v0, z0 = COL_GQ // gw, COL_GK // gw, COL_GV // gw, COL_Z // gw
    kc0, vc0 = (COL_GK - CTX_COL0) // gw, (COL_GV - CTX_COL0) // gw
    kern = functools.partial(_gdn_kernel, hb=hb, ncc=ncc, ncl=ncl)
    return pl.pallas_call(
        kern,
        grid=(b, ng),
        in_specs=[
            pl.BlockSpec((n, gw), lambda i, g: (i, q0 + g)),
            pl.BlockSpec((n, gw), lambda i, g: (i, k0 + g)),
            pl.BlockSpec((n, gw), lambda i, g: (i, v0 + g)),
            pl.BlockSpec((cl, gw), lambda i, g: (i, kc0 + g)),
            pl.BlockSpec((cl, gw), lambda i, g: (i, vc0 + g)),
            pl.BlockSpec((n, gw), lambda i, g: (i, z0 + g)),
            pl.BlockSpec((1, 1) + rowpack.shape[2:], lambda i, g: (i, g, 0, 0)),
            pl.BlockSpec((1, 1) + colpack.shape[2:], lambda i, g: (i, g, 0, 0)),
            pl.BlockSpec((1, LANES), lambda i, g: (0, 0)),
        ],
        out_specs=pl.BlockSpec((n, gw), lambda i, g: (i, g)),
        out_shape=jax.ShapeDtypeStruct((b * n, GDN_HEADS * LANES), BF16),
        scratch_shapes=[
            pltpu.VMEM((2 * hb, nt, LANES), F32),
            pltpu.VMEM((2 * hb, 2 * nt, LANES), BF16),
            pltpu.VMEM((2 * hb, nt, LANES), BF16),
            pltpu.VMEM((2 * hb, nt, LANES), BF16),
            pltpu.VMEM((2 * hb, GDN_DK, GDN_DV), F32),
            pltpu.VMEM((2 * hb, n, LANES), F32),
        ],
        compiler_params=pltpu.CompilerParams(dimension_semantics=("arbitrary", "arbitrary"),
                                             vmem_limit_bytes=VMEM_LIMIT),
    )(p_lat, p_lat, p_lat, p_ctx, p_ctx, p_lat, rowpack, colpack, nw)


def _merge_kernel(x_ref, a_ref, g_ref, ga_ref, gd_ref, g1_ref, wpa_ref, wpd_ref, wout_ref, o_ref):
    ya = _dot(a_ref[...], wpa_ref[...])
    yd = _dot(g_ref[...], wpd_ref[...])
    y = ga_ref[...].astype(F32) * ya + gd_ref[...].astype(F32) * yd
    o_ref[...] = x_ref[...] + g1_ref[0] * _dot(y.astype(BF16), wout_ref[...])


def _merge(x2, attn, gdn, p_lat, g1, wpa, wpd, wout, *, n, tm):
    m, d = x2.shape
    per_seq = n // tm
    return pl.pallas_call(
        _merge_kernel,
        grid=(m // tm,),
        in_specs=[
            pl.BlockSpec((tm, d), lambda i: (i, 0)),
            pl.BlockSpec((tm, d), lambda i: (i, 0)),
            pl.BlockSpec((tm, d), lambda i: (i, 0)),
            pl.BlockSpec((tm, d), lambda i: (i, COL_GATES // d)),
            pl.BlockSpec((tm, d), lambda i: (i, COL_GATES // d + 1)),
            pl.BlockSpec((1, 1, d), lambda i: (i // per_seq, 0, 0)),
            pl.BlockSpec((d, d), lambda i: (0, 0)),
            pl.BlockSpec((d, d), lambda i: (0, 0)),
            pl.BlockSpec((d, d), lambda i: (0, 0)),
        ],
        out_specs=pl.BlockSpec((tm, d), lambda i: (i, 0)),
        out_shape=jax.ShapeDtypeStruct((m, d), F32),
        compiler_params=pltpu.CompilerParams(dimension_semantics=("arbitrary",), vmem_limit_bytes=VMEM_LIMIT),
    )(x2, attn, gdn, p_lat, p_lat, g1, wpa, wpd, wout)


def _ffn_kernel(x_ref, sh_ref, sc_ref, g2_ref, wg_ref, wv_ref, cwg_ref, cwv_ref, cbg_ref, cbv_ref, wd_ref, fw_ref,
                o_ref, h_ref, *, seq_len):
    j = pl.program_id(1)
    tm = h_ref.shape[0]
    assert tm % ROW_BLOCK == 0

    @pl.when(j == 0)
    def _():
        x = x_ref[...]
        r = lax.rsqrt(jnp.mean(x * x, axis=-1, keepdims=True) + NORM_EPS)
        h_ref[...] = ((x * r) * (1.0 + sc_ref[0]) + sh_ref[0]).astype(BF16)
        o_ref[...] = jnp.zeros_like(o_ref)

    def produce(r):
        _, lo, hi = _block_rows(r, tm, seq_len)
        h = h_ref[lo:hi, :]
        return _dot(h, wg_ref[...]), _dot(h, wv_ref[...])

    def consume(r, up):
        r0, lo, _ = _block_rows(r, tm, seq_len)
        ug = _conv3_block(up[0], cwg_ref, r0, lo, seq_len) + cbg_ref[...]
        uv = _conv3_block(up[1], cwv_ref, r0, lo, seq_len) + cbv_ref[...]
        o_ref[r0:r0 + ROW_BLOCK, :] += _dot((_silu(ug) * uv).astype(BF16), wd_ref[...])

    _skewed(tm // ROW_BLOCK, produce, consume)

    @pl.when(j == pl.num_programs(1) - 1)
    def _():
        y = x_ref[...] + g2_ref[0] * o_ref[...]
        o_ref[...] = y * lax.rsqrt(jnp.mean(y * y, axis=-1, keepdims=True) + NORM_EPS) * fw_ref[...]


def _ffn(x1, sh2, sc2, g2, wup, cw, cb, wdown, fw, *, n, tf):
    m, d = x1.shape
    dff = wdown.shape[0]
    nj = dff // tf
    mod_map = lambda i, j: (i, 0, 0)
    return pl.pallas_call(
        functools.partial(_ffn_kernel, seq_len=n),
        grid=(m // n, nj),
        in_specs=[
            pl.BlockSpec((n, d), lambda i, j: (i, 0)),
            pl.BlockSpec((1, 1, d), mod_map),
            pl.BlockSpec((1, 1, d), mod_map),
            pl.BlockSpec((1, 1, d), mod_map),
            pl.BlockSpec((d, tf), lambda i, j: (0, j)),
            pl.BlockSpec((d, tf), lambda i, j: (0, j + nj)),
            pl.BlockSpec((3, tf), lambda i, j: (0, j)),
            pl.BlockSpec((3, tf), lambda i, j: (0, j + nj)),
            pl.BlockSpec((1, tf), lambda i, j: (0, j)),
            pl.BlockSpec((1, tf), lambda i, j: (0, j + nj)),
            pl.BlockSpec((tf, d), lambda i, j: (j, 0)),
            pl.BlockSpec((1, d), lambda i, j: (0, 0)),
        ],
        out_specs=pl.BlockSpec((n, d), lambda i, j: (i, 0)),
        out_shape=jax.ShapeDtypeStruct((m, d), F32),
        scratch_shapes=[pltpu.VMEM((n, d), BF16)],
        compiler_params=pltpu.CompilerParams(dimension_semantics=("arbitrary", "arbitrary"),
                                             vmem_limit_bytes=VMEM_LIMIT),
    )(x1, sh2, sc2, g2, wup, wup, cw, cw, cb, cb, wdown, fw)


def _rope_tables(n):
    rows = n // GRID_W
    row_ids = jnp.broadcast_to(jnp.arange(rows, dtype=F32)[:, None], (rows, GRID_W)).reshape(n)
    col_ids = jnp.broadcast_to(jnp.arange(GRID_W, dtype=F32)[None, :], (rows, GRID_W)).reshape(n)
    inv_freq = ROPE_THETA ** (-jnp.arange(0, ROPE_AXIS_DIM, 2, dtype=F32) / ROPE_AXIS_DIM)
    ar = row_ids[:, None] * inv_freq
    ac = col_ids[:, None] * inv_freq
    cos = jnp.concatenate([jnp.cos(ar), jnp.cos(ar), jnp.cos(ac), jnp.cos(ac)], axis=-1)
    sin = jnp.concatenate([-jnp.sin(ar), jnp.sin(ar), -jnp.sin(ac), jnp.sin(ac)], axis=-1)
    return cos, sin


def kernel(x, c, ctx, c_ctx, w_mod, b_mod, w_in, q_norm_w, k_norm_w, conv_qkv_w, a_log, dt_bias, gdn_norm_w, w_pa,
           w_pd, w_out, w_up, ffn_conv_w, ffn_conv_b, w_down, final_norm_w):
    assert w_mod.shape[0] == 1, "single-layer block"
    b, n, d = x.shape
    cl = ctx.shape[1]
    hb = 2

    pad = (-(b + 1)) % 8
    cc = jnp.concatenate([c, c_ctx[None, :], jnp.zeros((pad, d), F32)], axis=0)
    mod = _modulation(cc, w_mod[0], b_mod[0][None, :])
    sh1, sc1, g1, sh2, sc2, g2 = [t[:, None, :] for t in jnp.split(mod[:b], 6, axis=-1)]
    csh1, csc1 = mod[b:b + 1, None, :d], mod[b:b + 1, None, d:2 * d]

    w = w_in[0]
    akv, qkv_w = ATTN_KV_HEADS * HEAD_DIM, 3 * GDN_HEADS * GDN_DK
    o_qkv = 2 * akv
    o_db = o_qkv + qkv_w
    o_aq = o_db + 4 * GDN_HEADS
    o_z = o_aq + ATTN_HEADS * HEAD_DIM
    o_gate = o_z + GDN_HEADS * GDN_DV
    w_main = jnp.concatenate([w[:, o_gate:], w[:, o_aq:o_z], w[:, o_z:o_gate], w[:, o_qkv:o_db], w[:, :o_qkv]],
                             axis=1).astype(BF16)
    w_small = jnp.concatenate([w[:, o_db:o_aq], jnp.zeros((d, LANES - 4 * GDN_HEADS), F32)], axis=1).astype(BF16)
    cos, sin = _rope_tables(n)
    x2 = x.reshape(b * n, d)
    common = (w_main, w_small, q_norm_w, k_norm_w, conv_qkv_w[0], cos, sin)
    p_lat, s_lat = _input_projection(x2, sh1, sc1, *common, tm=n, seq_len=n, tile0=0,
                                     ntiles=MAIN_COLS // PROJ_TILE, rope=True)
    p_ctx, s_ctx = _input_projection(ctx.reshape(b * cl, d), csh1, csc1, *common, tm=b * cl, seq_len=cl,
                                     tile0=CTX_COL0 // PROJ_TILE, ntiles=(MAIN_COLS - CTX_COL0) // PROJ_TILE,
                                     rope=False)

    ncc, ncl = cl // GDN_CHUNK, n // GDN_CHUNK
    nc = ncc + ncl
    ng = GDN_HEADS // hb
    nsm = 4 * GDN_HEADS
    small = jnp.concatenate([s_ctx[:, :nsm].reshape(b, cl, nsm), s_lat[:, :nsm].reshape(b, n, nsm)], axis=1)
    small = small.transpose(0, 2, 1).reshape(b, nsm * nc, GDN_CHUNK)
    alog_rows = jnp.repeat(a_log[0].reshape(-1), nc)[:, None]
    dtb_rows = jnp.repeat(dt_bias[0].reshape(-1), nc)[:, None]
    gq = _gates(small, alog_rows, dtb_rows).reshape(b, 6, 2, ng, hb, nc, GDN_CHUNK)
    rowpack = gq[:, jnp.array([1, 4, 5])].transpose(0, 3, 1, 2, 4, 5, 6).reshape(b, ng, 3 * 2 * hb * nc, GDN_CHUNK)
    colpack = gq[:, :4].transpose(0, 3, 5, 6, 1, 2, 4).reshape(b, ng, nc * GDN_CHUNK, 4 * 2 * hb)

    attn = _attention(p_lat, p_ctx, b=b, n=n, cl=cl, tq=min(n, 512))
    gdn = _gdn(p_lat, p_ctx, rowpack, colpack, gdn_norm_w, b=b, n=n, cl=cl, hb=hb)

    x1 = _merge(x2, attn, gdn, p_lat, g1, w_pa[0].astype(BF16), w_pd[0].astype(BF16), w_out[0].astype(BF16),
                n=n, tm=min(n, 1024))

    out = _ffn(x1, sh2, sc2, g2, w_up[0].astype(BF16), ffn_conv_w[0], ffn_conv_b[0][None, :],
               w_down[0].astype(BF16), final_norm_w[None, :], n=n, tf=256)
    return out.reshape(b, n, d)
```

```python
import functools
import math

import jax
import jax.numpy as jnp
from jax import lax
from jax.experimental import pallas as pl
from jax.experimental.pallas import tpu as pltpu

F32 = jnp.float32
BF16 = jnp.bfloat16

LANES = 128
GRID_W = 64
ATTN_HEADS = 8
ATTN_KV_HEADS = 2
ATTN_GROUP = ATTN_HEADS // ATTN_KV_HEADS
HEAD_DIM = 128
ROPE_AXIS_DIM = HEAD_DIM // 2
ROPE_THETA = 10000.0
GDN_HEADS = 8
GDN_DK = 128
GDN_DV = 128
GDN_CHUNK = 128
NORM_EPS = 1e-6
NEG_BIG = -1e30

PROJ_TILE = 512
COL_GATES = 0
COL_AQ = 2048
COL_Z = 3072
COL_GQ = 4096
COL_GK = 5120
COL_GV = 6144
COL_AK = 7168
COL_AV = 7424
MAIN_COLS = 7680
CTX_COL0 = COL_GQ
VMEM_LIMIT = 56 * 1024 * 1024


def _dot(a, b):
    return jnp.dot(a, b, preferred_element_type=F32)


def _dot_nt(a, b):
    return lax.dot_general(a, b, (((1,), (1,)), ((), ())), preferred_element_type=F32)


def _sigmoid(x):
    return 1.0 / (1.0 + jnp.exp(-x))


def _silu(x):
    return x * _sigmoid(x)


def _mod_kernel(c_ref, w_ref, b_ref, o_ref):
    s = _silu(c_ref[...])
    w = w_ref[...]
    s_hi = s.astype(BF16)
    s_lo = (s - s_hi.astype(F32)).astype(BF16)
    w_hi = w.astype(BF16)
    w_lo = (w - w_hi.astype(F32)).astype(BF16)
    o_ref[...] = _dot(s_hi, w_hi) + _dot(s_lo, w_hi) + _dot(s_hi, w_lo) + b_ref[...]


def _modulation(cc, w_mod, b_mod):
    rows, d = cc.shape
    n = w_mod.shape[1]
    tn = 1024
    return pl.pallas_call(
        _mod_kernel,
        grid=(n // tn,),
        in_specs=[pl.BlockSpec((rows, d), lambda j: (0, 0)),
                  pl.BlockSpec((d, tn), lambda j: (0, j)),
                  pl.BlockSpec((1, tn), lambda j: (0, j))],
        out_specs=pl.BlockSpec((rows, tn), lambda j: (0, j)),
        out_shape=jax.ShapeDtypeStruct((rows, n), F32),
    )(cc, w_mod, b_mod)


ROW_BLOCK = 256
ROW_HALO = 16


def _skewed(n, produce, consume):
    prev = produce(0)
    for r in range(1, n):
        cur = produce(r)
        consume(r - 1, prev)
        prev = cur
    consume(n - 1, prev)


def _block_rows(r, rb, tm, seq_len):
    r0 = r * rb
    halo = ROW_HALO if seq_len > rb else 0
    return r0, max(r0 - halo, 0), min(r0 + rb + halo, tm)


def _conv3_block(ext, w_ref, r0, lo, seq_len, rb):
    assert seq_len % rb == 0
    n_ext = ext.shape[0]
    off = r0 - lo
    prev = pltpu.roll(ext, 1, axis=0)[off:off + rb]
    nxt = pltpu.roll(ext, n_ext - 1, axis=0)[off:off + rb]
    row = lax.broadcasted_iota(jnp.int32, (rb, 1), 0)
    if r0 % seq_len == 0:
        prev = jnp.where(row == 0, 0.0, prev)
    if (r0 + rb) % seq_len == 0:
        nxt = jnp.where(row == rb - 1, 0.0, nxt)
    return prev * w_ref[0:1, :] + ext[off:off + rb] * w_ref[1:2, :] + nxt * w_ref[2:3, :]


def _rope(a, cos, sin):
    lane = lax.broadcasted_iota(jnp.int32, a.shape, 1)
    swapped = jnp.where((lane & 32) == 0, pltpu.roll(a, 96, axis=1), pltpu.roll(a, 32, axis=1))
    return a * cos + swapped * sin


def _inproj_kernel(x_ref, sh_ref, sc_ref, w_ref, ws_ref, qn_ref, kn_ref, cw_ref, cos_ref, sin_ref,
                   o_ref, os_ref, hx_ref, *, tile0, ntiles, seq_len, rope, rb):
    j = pl.program_id(1)
    tm = hx_ref.shape[0]
    assert tm % rb == 0

    @pl.when(j == 0)
    def _():
        x = x_ref[...]
        r = lax.rsqrt(jnp.mean(x * x, axis=-1, keepdims=True) + NORM_EPS)
        h = (x * r) * (1.0 + sc_ref[0]) + sh_ref[0]
        hb = h.astype(BF16)
        hx_ref[...] = hb
        os_ref[...] = _dot(hb, ws_ref[...])

    kind = j + tile0
    nh = PROJ_TILE // LANES

    def head(a, hh):
        return a[:, hh * LANES:(hh + 1) * LANES]

    def store_head(r0, hh, a):
        o_ref[r0:r0 + rb, hh * LANES:(hh + 1) * LANES] = a.astype(BF16)

    def qk_norm_rope(a, w, scale, r0):
        a = a * lax.rsqrt(jnp.mean(a * a, axis=-1, keepdims=True) + NORM_EPS) * w
        if rope:
            a = _rope(a, cos_ref[r0:r0 + rb, :], sin_ref[r0:r0 + rb, :])
        return a * scale if scale != 1.0 else a

    def ep_gates(acc, r0, lo):
        o_ref[r0:r0 + rb, :] = _sigmoid(acc).astype(BF16)

    def ep_attn_q(acc, r0, lo):
        for hh in range(nh):
            store_head(r0, hh, qk_norm_rope(head(acc, hh), qn_ref[...], HEAD_DIM ** -0.5, r0))

    def ep_plain(acc, r0, lo):
        o_ref[r0:r0 + rb, :] = acc.astype(BF16)

    def ep_gdn_qk(scale):
        def ep(ext, r0, lo):
            y = _silu(_conv3_block(ext, cw_ref, r0, lo, seq_len, rb))
            for hh in range(nh):
                a = head(y, hh)
                store_head(r0, hh, a * (lax.rsqrt(jnp.sum(a * a, axis=-1, keepdims=True) + NORM_EPS) * scale))
        return ep

    def ep_gdn_v(ext, r0, lo):
        o_ref[r0:r0 + rb, :] = _silu(_conv3_block(ext, cw_ref, r0, lo, seq_len, rb)).astype(BF16)

    def ep_attn_kv(acc, r0, lo):
        for hh in range(ATTN_KV_HEADS):
            store_head(r0, hh, qk_norm_rope(head(acc, hh), kn_ref[...], 1.0, r0))
        o_ref[r0:r0 + rb, ATTN_KV_HEADS * LANES:] = acc[:, ATTN_KV_HEADS * LANES:].astype(BF16)

    groups = [(0, 4, ep_gates, False), (4, 6, ep_attn_q, False), (6, 8, ep_plain, False),
              (8, 10, ep_gdn_qk(GDN_DK ** -0.5), True), (10, 12, ep_gdn_qk(1.0), True), (12, 14, ep_gdn_v, True),
              (14, 15, ep_attn_kv, False)]

    for k_lo, k_hi, epilogue, conv in groups:
        if max(k_lo, tile0) >= min(k_hi, tile0 + ntiles):
            continue

        def run(epilogue=epilogue, conv=conv):
            def rows(r):
                return _block_rows(r, rb, tm, seq_len) if conv else (r * rb, r * rb, (r + 1) * rb)

            def produce(r):
                _, lo, hi = rows(r)
                return _dot(hx_ref[lo:hi, :], w_ref[...])

            def consume(r, acc):
                r0, lo, _ = rows(r)
                epilogue(acc, r0, lo)

            _skewed(tm // rb, produce, consume)

        pl.when((kind >= k_lo) & (kind < k_hi))(run)


def _input_projection(x2, shift, scale, w_main, w_small, qn, kn, cw, cos, sin, *, tm, seq_len, tile0, ntiles, rope):
    m, d = x2.shape
    nrow = m // tm
    per_row_mod = shift.shape[0] > 1
    mod_map = (lambda i, j: (i, 0, 0)) if per_row_mod else (lambda i, j: (0, 0, 0))
    conv_tile0 = COL_GQ // PROJ_TILE
    n_conv_tiles = cw.shape[1] // PROJ_TILE
    kern = functools.partial(_inproj_kernel, tile0=tile0, ntiles=ntiles, seq_len=seq_len, rope=rope,
                             rb=min(2 * ROW_BLOCK, seq_len))
    return pl.pallas_call(
        kern,
        grid=(nrow, ntiles),
        in_specs=[
            pl.BlockSpec((tm, d), lambda i, j: (i, 0)),
            pl.BlockSpec((1, 1, d), mod_map),
            pl.BlockSpec((1, 1, d), mod_map),
            pl.BlockSpec((d, PROJ_TILE), lambda i, j: (0, j + tile0)),
            pl.BlockSpec((d, LANES), lambda i, j: (0, 0)),
            pl.BlockSpec((1, LANES), lambda i, j: (0, 0)),
            pl.BlockSpec((1, LANES), lambda i, j: (0, 0)),
            pl.BlockSpec((3, PROJ_TILE), lambda i, j: (0, jnp.clip(j + tile0 - conv_tile0, 0, n_conv_tiles - 1))),
            pl.BlockSpec(cos.shape, lambda i, j: (0, 0)),
            pl.BlockSpec(sin.shape, lambda i, j: (0, 0)),
        ],
        out_specs=[pl.BlockSpec((tm, PROJ_TILE), lambda i, j: (i, j)),
                   pl.BlockSpec((tm, LANES), lambda i, j: (i, 0))],
        out_shape=[jax.ShapeDtypeStruct((m, ntiles * PROJ_TILE), BF16),
                   jax.ShapeDtypeStruct((m, LANES), F32)],
        scratch_shapes=[pltpu.VMEM((tm, d), BF16)],
        compiler_params=pltpu.CompilerParams(dimension_semantics=("arbitrary", "arbitrary"),
                                             vmem_limit_bytes=VMEM_LIMIT),
    )(x2, shift, scale, w_main, w_small, qn, kn, cw, cos, sin)


def _gate_kernel(g_ref, alog_ref, dtb_ref, o_ref, *, nrows):
    db = g_ref[0, :nrows, :]
    da = g_ref[0, nrows:, :]
    beta = _sigmoid(db)
    z = da + dtb_ref[...]
    softplus = jnp.maximum(z, 0.0) + jnp.log(1.0 + jnp.exp(-jnp.abs(z)))
    la = -jnp.exp(alog_ref[...]) * softplus
    half = nrows // 2
    lane = lax.broadcasted_iota(jnp.int32, (half, LANES), 1)
    pre = la[:half]
    suf = la[half:]
    k = 1
    while k < LANES:
        pre = pre + jnp.where(lane >= k, pltpu.roll(pre, k, axis=1), 0.0)
        suf = suf + jnp.where(lane < LANES - k, pltpu.roll(suf, LANES - k, axis=1), 0.0)
        k *= 2
    gam = jnp.concatenate([pre, suf], axis=0)
    tot = jnp.sum(la, axis=-1, keepdims=True)
    eg = jnp.exp(gam)
    o_ref[0, 0] = beta
    o_ref[0, 1] = gam
    o_ref[0, 2] = eg
    o_ref[0, 3] = beta * eg
    o_ref[0, 4] = jnp.exp(tot - gam)
    o_ref[0, 5] = jnp.broadcast_to(jnp.exp(tot), gam.shape)


def _gates(g, alog_rows, dtb_rows):
    b, r2, _ = g.shape
    nrows = r2 // 2
    return pl.pallas_call(
        functools.partial(_gate_kernel, nrows=nrows),
        grid=(b,),
        in_specs=[pl.BlockSpec((1, r2, LANES), lambda i: (i, 0, 0)),
                  pl.BlockSpec((nrows, 1), lambda i: (0, 0)),
                  pl.BlockSpec((nrows, 1), lambda i: (0, 0))],
        out_specs=pl.BlockSpec((1, 6, nrows, LANES), lambda i: (i, 0, 0, 0)),
        out_shape=jax.ShapeDtypeStruct((b, 6, nrows, LANES), F32),
    )(g, alog_rows, dtb_rows)


def _attn_kernel(q_ref, kc_ref, vc_ref, kx_ref, vx_ref, o_ref):
    kc = kc_ref[...]
    kx = kx_ref[...]
    vc = vc_ref[...]
    vx = vx_ref[...]
    tq = q_ref.shape[0]
    rq = min(tq, 2 * ROW_BLOCK)
    nblk = tq // rq

    def unit(u):
        hh, blk = divmod(u, nblk)
        return slice(blk * rq, (blk + 1) * rq), slice(hh * LANES, (hh + 1) * LANES)

    def scores(u):
        rows, cols = unit(u)
        q = q_ref[rows, cols]
        return _dot_nt(q, kc), _dot_nt(q, kx)

    def softmax_pv(u, s):
        rows, cols = unit(u)
        sc, sx = s
        m = jnp.maximum(jnp.max(sc, axis=-1, keepdims=True), jnp.max(sx, axis=-1, keepdims=True))
        pc = jnp.exp(sc - m)
        px = jnp.exp(sx - m)
        denom = jnp.sum(pc, axis=-1, keepdims=True) + jnp.sum(px, axis=-1, keepdims=True)
        o = _dot(pc.astype(BF16), vc) + _dot(px.astype(BF16), vx)
        o_ref[rows, cols] = (o / denom).astype(BF16)

    _skewed(ATTN_GROUP * nblk, scores, softmax_pv)


def _attention(p_lat, p_ctx, *, b, n, cl, tq):
    gw = ATTN_GROUP * LANES
    nq = n // tq
    q0 = COL_AQ // gw
    kx0, vx0 = COL_AK // LANES, COL_AV // LANES
    kc0, vc0 = (COL_AK - CTX_COL0) // LANES, (COL_AV - CTX_COL0) // LANES
    return pl.pallas_call(
        _attn_kernel,
        grid=(b, ATTN_KV_HEADS, nq),
        in_specs=[
            pl.BlockSpec((tq, gw), lambda i, h, t: (i * nq + t, q0 + h)),
            pl.BlockSpec((cl, LANES), lambda i, h, t: (i, kc0 + h)),
            pl.BlockSpec((cl, LANES), lambda i, h, t: (i, vc0 + h)),
            pl.BlockSpec((n, LANES), lambda i, h, t: (i, kx0 + h)),
            pl.BlockSpec((n, LANES), lambda i, h, t: (i, vx0 + h)),
        ],
        out_specs=pl.BlockSpec((tq, gw), lambda i, h, t: (i * nq + t, h)),
        out_shape=jax.ShapeDtypeStruct((b * n, ATTN_HEADS * LANES), BF16),
        compiler_params=pltpu.CompilerParams(dimension_semantics=("arbitrary", "arbitrary", "arbitrary"),
                                             vmem_limit_bytes=VMEM_LIMIT),
    )(p_lat, p_ctx, p_ctx, p_lat, p_lat)


def _gdn_kernel(ql_ref, kl_ref, vl_ref, kc_ref, vc_ref, z_ref, row_ref, col_ref, nw_ref, o_ref,
                u_ref, wq_ref, kdt_ref, p_ref, s_ref, oacc_ref, *, hb, ncc, ncl):
    c_sz = GDN_CHUNK
    nc = ncc + ncl
    ri = lax.broadcasted_iota(jnp.int32, (c_sz, c_sz), 0)
    ci = lax.broadcasted_iota(jnp.int32, (c_sz, c_sz), 1)
    eye = (ri == ci).astype(F32)
    levels = int(math.log2(c_sz))
    cu_ctx = math.gcd(ncc, 2)
    cu_lat = math.gcd(ncl, 4)

    def tri_masks(d):
        lo, hi = (ci, ri) if d == 0 else (ri, ci)
        incl = hi >= lo
        strict = hi > lo
        pair = []
        for lv in range(levels):
            bh = hi >> lv
            bl = lo >> lv
            pair.append(((bh & 1) == 1) & (bl == bh - 1))
        return incl, strict, pair

    masks = [tri_masks(0), tri_masks(1)]

    def row_q(qi, d, hh, c):
        base = ((qi * 2 + d) * hb + hh) * nc
        return row_ref[0, 0, pl.ds(base + c, 1), :]

    def col_q(qi, d, hh, r0):
        lane = (qi * 2 + d) * hb + hh
        return col_ref[0, 0, pl.ds(r0, c_sz), lane:lane + 1]

    def prep(chunks):
        inst = []
        for c, k, v, q in chunks:
            r0 = pl.multiple_of(c * c_sz, c_sz)
            rq = pl.multiple_of(c * (2 * c_sz), 2 * c_sz)
            for hh in range(hb):
                kh = k[hh]
                kk = _dot_nt(kh, kh)
                qk = None if q is None else _dot_nt(q[hh], kh)
                for d in range(2):
                    inst.append(dict(c=c, r0=r0, rq=rq, hh=hh, d=d, hd=hh * 2 + d, kh=kh, vh=v[hh], kk=kk, qk=qk,
                                     qh=None if q is None else q[hh]))
        for it in inst:
            d, hh, r0 = it["d"], it["hh"], it["r0"]
            incl, strict, pair = masks[d]
            beta_c = col_q(0, d, hh, r0)
            e = jnp.exp(jnp.where(incl, col_q(1, d, hh, r0) - row_q(0, d, hh, it["c"]), NEG_BIG))
            lm = jnp.where(strict, beta_c * e * it["kk"], 0.0)
            it["t"] = eye - jnp.where(pair[0], lm, 0.0)
            it["ms"] = [jnp.where(pair[lv], lm, 0.0).astype(BF16) for lv in range(1, levels)]
            kf = it["kh"].astype(F32)
            it["rhs"] = jnp.concatenate([beta_c * it["vh"].astype(F32), col_q(3, d, hh, r0) * kf], axis=1).astype(BF16)
            kdt_ref[it["hd"], pl.ds(r0, c_sz), :] = (kf.T * row_q(1, d, hh, it["c"])).astype(BF16)
            if it["qk"] is not None:
                wq_ref[it["hd"], pl.ds(it["rq"] + c_sz, c_sz), :] = (
                    it["qh"].astype(F32) * col_q(2, d, hh, r0)).astype(BF16)
                p_ref[it["hd"], pl.ds(r0, c_sz), :] = (e * it["qk"]).astype(BF16)
        for lv in range(levels - 1):
            xs = [_dot(it["ms"][lv], it["t"].astype(BF16)).astype(BF16) for it in inst]
            for it, xm in zip(inst, xs):
                it["t"] = it["t"] - _dot(it["t"].astype(BF16), xm)
        uws = [_dot(it["t"].astype(BF16), it["rhs"]) for it in inst]
        for it, uw in zip(inst, uws):
            u_ref[it["hd"], pl.ds(it["r0"], c_sz), :] = uw[:, :GDN_DV]
            wq_ref[it["hd"], pl.ds(it["rq"], c_sz), :] = uw[:, GDN_DV:].astype(BF16)

    def split_heads(ref, r0):
        return [ref[pl.ds(r0, c_sz), hh * LANES:(hh + 1) * LANES] for hh in range(hb)]

    def prep_ctx(i, carry):
        chunks = []
        for j in range(cu_ctx):
            c = i * cu_ctx + j
            r0 = pl.multiple_of(c * c_sz, c_sz)
            chunks.append((c, split_heads(kc_ref, r0), split_heads(vc_ref, r0), None))
        prep(chunks)
        return carry

    def prep_lat(i, carry):
        chunks = []
        for j in range(cu_lat):
            c = i * cu_lat + j
            r0 = pl.multiple_of(c * c_sz, c_sz)
            chunks.append((c + ncc, split_heads(kl_ref, r0), split_heads(vl_ref, r0), split_heads(ql_ref, r0)))
        prep(chunks)
        return carry

    lax.fori_loop(0, ncc // cu_ctx, prep_ctx, 0)
    lax.fori_loop(0, ncl // cu_lat, prep_lat, 0)

    s_ref[...] = jnp.zeros_like(s_ref)

    def scan_steps(steps):
        st = []
        for hh, d, c, out_row in steps:
            hd = hh * 2 + d
            r0 = pl.multiple_of(c * c_sz, c_sz)
            rq = pl.multiple_of(c * (2 * c_sz), 2 * c_sz)
            s = s_ref[hd]
            rows = c_sz if out_row is None else 2 * c_sz
            st.append(dict(hd=hd, r0=r0, s=s, gl=row_q(2, d, hh, c), out_row=out_row,
                           wqs=_dot(wq_ref[hd, pl.ds(rq, rows), :], s.astype(BF16))))
        for it in st:
            it["delta"] = (u_ref[it["hd"], pl.ds(it["r0"], c_sz), :] - it["wqs"][:c_sz]).astype(BF16)
        for it in st:
            s_ref[it["hd"]] = it["gl"] * it["s"] + _dot(kdt_ref[it["hd"], pl.ds(it["r0"], c_sz), :], it["delta"])
        for it in st:
            if it["out_row"] is not None:
                oacc_ref[it["hd"], pl.ds(it["out_row"], c_sz), :] = (
                    it["wqs"][c_sz:] + _dot(p_ref[it["hd"], pl.ds(it["r0"], c_sz), :], it["delta"]))

    def scan_ctx(i, carry):
        steps = []
        for hh in range(hb):
            steps += [(hh, 0, i, None), (hh, 1, ncc - 1 - i, None)]
        scan_steps(steps)
        return carry

    def scan_lat(i, carry):
        steps = []
        for hh in range(hb):
            steps += [(hh, 0, ncc + i, pl.multiple_of(i * c_sz, c_sz)),
                      (hh, 1, ncc + ncl - 1 - i, pl.multiple_of((ncl - 1 - i) * c_sz, c_sz))]
        scan_steps(steps)
        return carry

    lax.fori_loop(0, ncc, scan_ctx, 0)
    lax.fori_loop(0, ncl, scan_lat, 0)

    def finish(i, carry):
        r0 = pl.multiple_of(i * c_sz, c_sz)
        for hh in range(hb):
            o = oacc_ref[2 * hh, pl.ds(r0, c_sz), :] + oacc_ref[2 * hh + 1, pl.ds(r0, c_sz), :]
            o = o * lax.rsqrt(jnp.mean(o * o, axis=-1, keepdims=True) + NORM_EPS) * nw_ref[...]
            zf = z_ref[pl.ds(r0, c_sz), hh * LANES:(hh + 1) * LANES].astype(F32)
            o_ref[pl.ds(r0, c_sz), hh * LANES:(hh + 1) * LANES] = (o * _silu(zf)).astype(BF16)
        return carry

    lax.fori_loop(0, ncl, finish, 0)


def _gdn(p_lat, p_ctx, rowpack, colpack, nw, *, b, n, cl, hb):
    gw = hb * LANES
    ng = GDN_HEADS // hb
    ncc, ncl = cl // GDN_CHUNK, n // GDN_CHUNK
    nc = ncc + ncl
    nt = nc * GDN_CHUNK
    q0, k0, v0, z0 = COL_GQ // gw, COL_GK // gw, COL_GV // gw, COL_Z // gw
    kc0, vc0 = (COL_GK - CTX_COL0) // gw, (COL_GV - CTX_COL0) // gw
    kern = functools.partial(_gdn_kernel, hb=hb, ncc=ncc, ncl=ncl)
    return pl.pallas_call(
        kern,
        grid=(b, ng),
        in_specs=[
            pl.BlockSpec((n, gw), lambda i, g: (i, q0 + g)),
            pl.BlockSpec((n, gw), lambda i, g: (i, k0 + g)),
            pl.BlockSpec((n, gw), lambda i, g: (i, v0 + g)),
            pl.BlockSpec((cl, gw), lambda i, g: (i, kc0 + g)),
            pl.BlockSpec((cl, gw), lambda i, g: (i, vc0 + g)),
            pl.BlockSpec((n, gw), lambda i, g: (i, z0 + g)),
            pl.BlockSpec((1, 1) + rowpack.shape[2:], lambda i, g: (i, g, 0, 0)),
            pl.BlockSpec((1, 1) + colpack.shape[2:], lambda i, g: (i, g, 0, 0)),
            pl.BlockSpec((1, LANES), lambda i, g: (0, 0)),
        ],
        out_specs=pl.BlockSpec((n, gw), lambda i, g: (i, g)),
        out_shape=jax.ShapeDtypeStruct((b * n, GDN_HEADS * LANES), BF16),
        scratch_shapes=[
            pltpu.VMEM((2 * hb, nt, LANES), F32),
            pltpu.VMEM((2 * hb, 2 * nt, LANES), BF16),
            pltpu.VMEM((2 * hb, nt, LANES), BF16),
            pltpu.VMEM((2 * hb, nt, LANES), BF16),
            pltpu.VMEM((2 * hb, GDN_DK, GDN_DV), F32),
            pltpu.VMEM((2 * hb, n, LANES), F32),
        ],
        compiler_params=pltpu.CompilerParams(dimension_semantics=("arbitrary", "arbitrary"),
                                             vmem_limit_bytes=VMEM_LIMIT),
    )(p_lat, p_lat, p_lat, p_ctx, p_ctx, p_lat, rowpack, colpack, nw)


def _merge_kernel(x_ref, a_ref, g_ref, ga_ref, gd_ref, g1_ref, wpa_ref, wpd_ref, wout_ref, o_ref):
    ya = _dot(a_ref[...], wpa_ref[...])
    yd = _dot(g_ref[...], wpd_ref[...])
    y = ga_ref[...].astype(F32) * ya + gd_ref[...].astype(F32) * yd
    o_ref[...] = x_ref[...] + g1_ref[0] * _dot(y.astype(BF16), wout_ref[...])


def _merge(x2, attn, gdn, p_lat, g1, wpa, wpd, wout, *, n, tm):
    m, d = x2.shape
    per_seq = n // tm
    return pl.pallas_call(
        _merge_kernel,
        grid=(m // tm,),
        in_specs=[
            pl.BlockSpec((tm, d), lambda i: (i, 0)),
            pl.BlockSpec((tm, d), lambda i: (i, 0)),
            pl.BlockSpec((tm, d), lambda i: (i, 0)),
            pl.BlockSpec((tm, d), lambda i: (i, COL_GATES // d)),
            pl.BlockSpec((tm, d), lambda i: (i, COL_GATES // d + 1)),
            pl.BlockSpec((1, 1, d), lambda i: (i // per_seq, 0, 0)),
            pl.BlockSpec((d, d), lambda i: (0, 0)),
            pl.BlockSpec((d, d), lambda i: (0, 0)),
            pl.BlockSpec((d, d), lambda i: (0, 0)),
        ],
        out_specs=pl.BlockSpec((tm, d), lambda i: (i, 0)),
        out_shape=jax.ShapeDtypeStruct((m, d), F32),
        compiler_params=pltpu.CompilerParams(dimension_semantics=("arbitrary",), vmem_limit_bytes=VMEM_LIMIT),
    )(x2, attn, gdn, p_lat, p_lat, g1, wpa, wpd, wout)


def _ffn_kernel(x_ref, sh_ref, sc_ref, g2_ref, wg_ref, wv_ref, cwg_ref, cwv_ref, cbg_ref, cbv_ref, wd_ref, fw_ref,
                o_ref, h_ref, *, seq_len, rb):
    j = pl.program_id(1)
    tm = h_ref.shape[0]
    assert tm % rb == 0

    @pl.when(j == 0)
    def _():
        x = x_ref[...]
        r = lax.rsqrt(jnp.mean(x * x, axis=-1, keepdims=True) + NORM_EPS)
        h_ref[...] = ((x * r) * (1.0 + sc_ref[0]) + sh_ref[0]).astype(BF16)
        o_ref[...] = jnp.zeros_like(o_ref)

    def produce(r):
        _, lo, hi = _block_rows(r, rb, tm, seq_len)
        h = h_ref[lo:hi, :]
        return _dot(h, wg_ref[...]), _dot(h, wv_ref[...])

    def consume(r, up):
        r0, lo, _ = _block_rows(r, rb, tm, seq_len)
        ug = _conv3_block(up[0], cwg_ref, r0, lo, seq_len, rb) + cbg_ref[...]
        uv = _conv3_block(up[1], cwv_ref, r0, lo, seq_len, rb) + cbv_ref[...]
        o_ref[r0:r0 + rb, :] += _dot((_silu(ug) * uv).astype(BF16), wd_ref[...])

    _skewed(tm // rb, produce, consume)

    @pl.when(j == pl.num_programs(1) - 1)
    def _():
        y = x_ref[...] + g2_ref[0] * o_ref[...]
        o_ref[...] = y * lax.rsqrt(jnp.mean(y * y, axis=-1, keepdims=True) + NORM_EPS) * fw_ref[...]


def _ffn(x1, sh2, sc2, g2, wup, cw, cb, wdown, fw, *, n, tf, rb):
    m, d = x1.shape
    dff = wdown.shape[0]
    nj = dff // tf
    mod_map = lambda i, j: (i, 0, 0)
    return pl.pallas_call(
        functools.partial(_ffn_kernel, seq_len=n, rb=min(rb, n)),
        grid=(m // n, nj),
        in_specs=[
            pl.BlockSpec((n, d), lambda i, j: (i, 0)),
            pl.BlockSpec((1, 1, d), mod_map),
            pl.BlockSpec((1, 1, d), mod_map),
            pl.BlockSpec((1, 1, d), mod_map),
            pl.BlockSpec((d, tf), lambda i, j: (0, j)),
            pl.BlockSpec((d, tf), lambda i, j: (0, j + nj)),
            pl.BlockSpec((3, tf), lambda i, j: (0, j)),
            pl.BlockSpec((3, tf), lambda i, j: (0, j + nj)),
            pl.BlockSpec((1, tf), lambda i, j: (0, j)),
            pl.BlockSpec((1, tf), lambda i, j: (0, j + nj)),
            pl.BlockSpec((tf, d), lambda i, j: (j, 0)),
            pl.BlockSpec((1, d), lambda i, j: (0, 0)),
        ],
        out_specs=pl.BlockSpec((n, d), lambda i, j: (i, 0)),
        out_shape=jax.ShapeDtypeStruct((m, d), F32),
        scratch_shapes=[pltpu.VMEM((n, d), BF16)],
        compiler_params=pltpu.CompilerParams(dimension_semantics=("arbitrary", "arbitrary"),
                                             vmem_limit_bytes=VMEM_LIMIT),
    )(x1, sh2, sc2, g2, wup, wup, cw, cw, cb, cb, wdown, fw)


def _rope_tables(n):
    rows = n // GRID_W
    row_ids = jnp.broadcast_to(jnp.arange(rows, dtype=F32)[:, None], (rows, GRID_W)).reshape(n)
    col_ids = jnp.broadcast_to(jnp.arange(GRID_W, dtype=F32)[None, :], (rows, GRID_W)).reshape(n)
    inv_freq = ROPE_THETA ** (-jnp.arange(0, ROPE_AXIS_DIM, 2, dtype=F32) / ROPE_AXIS_DIM)
    ar = row_ids[:, None] * inv_freq
    ac = col_ids[:, None] * inv_freq
    cos = jnp.concatenate([jnp.cos(ar), jnp.cos(ar), jnp.cos(ac), jnp.cos(ac)], axis=-1)
    sin = jnp.concatenate([-jnp.sin(ar), jnp.sin(ar), -jnp.sin(ac), jnp.sin(ac)], axis=-1)
    return cos, sin


def kernel(x, c, ctx, c_ctx, w_mod, b_mod, w_in, q_norm_w, k_norm_w, conv_qkv_w, a_log, dt_bias, gdn_norm_w, w_pa,
           w_pd, w_out, w_up, ffn_conv_w, ffn_conv_b, w_down, final_norm_w):
    assert w_mod.shape[0] == 1, "single-layer block"
    b, n, d = x.shape
    cl = ctx.shape[1]
    hb = 2

    pad = (-(b + 1)) % 8
    cc = jnp.concatenate([c, c_ctx[None, :], jnp.zeros((pad, d), F32)], axis=0)
    mod = _modulation(cc, w_mod[0], b_mod[0][None, :])
    sh1, sc1, g1, sh2, sc2, g2 = [t[:, None, :] for t in jnp.split(mod[:b], 6, axis=-1)]
    csh1, csc1 = mod[b:b + 1, None, :d], mod[b:b + 1, None, d:2 * d]

    w = w_in[0]
    akv, qkv_w = ATTN_KV_HEADS * HEAD_DIM, 3 * GDN_HEADS * GDN_DK
    o_qkv = 2 * akv
    o_db = o_qkv + qkv_w
    o_aq = o_db + 4 * GDN_HEADS
    o_z = o_aq + ATTN_HEADS * HEAD_DIM
    o_gate = o_z + GDN_HEADS * GDN_DV
    w_main = jnp.concatenate([w[:, o_gate:], w[:, o_aq:o_z], w[:, o_z:o_gate], w[:, o_qkv:o_db], w[:, :o_qkv]],
                             axis=1).astype(BF16)
    w_small = jnp.concatenate([w[:, o_db:o_aq], jnp.zeros((d, LANES - 4 * GDN_HEADS), F32)], axis=1).astype(BF16)
    cos, sin = _rope_tables(n)
    x2 = x.reshape(b * n, d)
    common = (w_main, w_small, q_norm_w, k_norm_w, conv_qkv_w[0], cos, sin)
    p_lat, s_lat = _input_projection(x2, sh1, sc1, *common, tm=n, seq_len=n, tile0=0,
                                     ntiles=MAIN_COLS // PROJ_TILE, rope=True)
    p_ctx, s_ctx = _input_projection(ctx.reshape(b * cl, d), csh1, csc1, *common, tm=b * cl, seq_len=cl,
                                     tile0=CTX_COL0 // PROJ_TILE, ntiles=(MAIN_COLS - CTX_COL0) // PROJ_TILE,
                                     rope=False)

    ncc, ncl = cl // GDN_CHUNK, n // GDN_CHUNK
    nc = ncc + ncl
    ng = GDN_HEADS // hb
    nsm = 4 * GDN_HEADS
    small = jnp.concatenate([s_ctx[:, :nsm].reshape(b, cl, nsm), s_lat[:, :nsm].reshape(b, n, nsm)], axis=1)
    small = small.transpose(0, 2, 1).reshape(b, nsm * nc, GDN_CHUNK)
    alog_rows = jnp.repeat(a_log[0].reshape(-1), nc)[:, None]
    dtb_rows = jnp.repeat(dt_bias[0].reshape(-1), nc)[:, None]
    gq = _gates(small, alog_rows, dtb_rows).reshape(b, 6, 2, ng, hb, nc, GDN_CHUNK)
    rowpack = gq[:, jnp.array([1, 4, 5])].transpose(0, 3, 1, 2, 4, 5, 6).reshape(b, ng, 3 * 2 * hb * nc, GDN_CHUNK)
    colpack = gq[:, :4].transpose(0, 3, 5, 6, 1, 2, 4).reshape(b, ng, nc * GDN_CHUNK, 4 * 2 * hb)

    attn = _attention(p_lat, p_ctx, b=b, n=n, cl=cl, tq=min(n, 1024))
    gdn = _gdn(p_lat, p_ctx, rowpack, colpack, gdn_norm_w, b=b, n=n, cl=cl, hb=hb)

    x1 = _merge(x2, attn, gdn, p_lat, g1, w_pa[0].astype(BF16), w_pd[0].astype(BF16), w_out[0].astype(BF16),
                n=n, tm=min(n, 1024))

    out = _ffn(x1, sh2, sc2, g2, w_up[0].astype(BF16), ffn_conv_w[0], ffn_conv_b[0][None, :],
               w_down[0].astype(BF16), final_norm_w[None, :], n=n, tf=256, rb=4 * ROW_BLOCK)
    return out.reshape(b, n, d)
```

```python
import functools
import math

import jax
import jax.numpy as jnp
from jax import lax
from jax.experimental import pallas as pl
from jax.experimental.pallas import tpu as pltpu

F32 = jnp.float32
BF16 = jnp.bfloat16

LANES = 128
GRID_W = 64
ATTN_HEADS = 8
ATTN_KV_HEADS = 2
ATTN_GROUP = ATTN_HEADS // ATTN_KV_HEADS
HEAD_DIM = 128
ROPE_AXIS_DIM = HEAD_DIM // 2
ROPE_THETA = 10000.0
GDN_HEADS = 8
GDN_DK = 128
GDN_DV = 128
GDN_CHUNK = 128
NORM_EPS = 1e-6
NEG_BIG = -1e30

PROJ_TILE = 512
COL_GATES = 0
COL_AQ = 2048
COL_Z = 3072
COL_GQ = 4096
COL_GK = 5120
COL_GV = 6144
COL_AK = 7168
COL_AV = 7424
MAIN_COLS = 7680
CTX_COL0 = COL_GQ
VMEM_LIMIT = 56 * 1024 * 1024


def _dot(a, b):
    return jnp.dot(a, b, preferred_element_type=F32)


def _dot_nt(a, b):
    return lax.dot_general(a, b, (((1,), (1,)), ((), ())), preferred_element_type=F32)


def _sigmoid(x):
    return 1.0 / (1.0 + jnp.exp(-x))


def _silu(x):
    return x * _sigmoid(x)


def _mod_kernel(c_ref, w_ref, b_ref, o_ref):
    s = _silu(c_ref[...])
    w = w_ref[...]
    s_hi = s.astype(BF16)
    s_lo = (s - s_hi.astype(F32)).astype(BF16)
    w_hi = w.astype(BF16)
    w_lo = (w - w_hi.astype(F32)).astype(BF16)
    o_ref[...] = _dot(s_hi, w_hi) + _dot(s_lo, w_hi) + _dot(s_hi, w_lo) + b_ref[...]


def _modulation(cc, w_mod, b_mod):
    rows, d = cc.shape
    n = w_mod.shape[1]
    tn = 1024
    return pl.pallas_call(
        _mod_kernel,
        grid=(n // tn,),
        in_specs=[pl.BlockSpec((rows, d), lambda j: (0, 0)),
                  pl.BlockSpec((d, tn), lambda j: (0, j)),
                  pl.BlockSpec((1, tn), lambda j: (0, j))],
        out_specs=pl.BlockSpec((rows, tn), lambda j: (0, j)),
        out_shape=jax.ShapeDtypeStruct((rows, n), F32),
    )(cc, w_mod, b_mod)


ROW_BLOCK = 256
ROW_HALO = 16


def _skewed(n, produce, consume):
    prev = produce(0)
    for r in range(1, n):
        cur = produce(r)
        consume(r - 1, prev)
        prev = cur
    consume(n - 1, prev)


def _block_rows(r, rb, tm, seq_len):
    r0 = r * rb
    halo = ROW_HALO if seq_len > rb else 0
    return r0, max(r0 - halo, 0), min(r0 + rb + halo, tm)


def _conv3_block(ext, w_ref, r0, lo, seq_len, rb):
    assert seq_len % rb == 0
    n_ext = ext.shape[0]
    off = r0 - lo
    prev = pltpu.roll(ext, 1, axis=0)[off:off + rb]
    nxt = pltpu.roll(ext, n_ext - 1, axis=0)[off:off + rb]
    row = lax.broadcasted_iota(jnp.int32, (rb, 1), 0)
    if r0 % seq_len == 0:
        prev = jnp.where(row == 0, 0.0, prev)
    if (r0 + rb) % seq_len == 0:
        nxt = jnp.where(row == rb - 1, 0.0, nxt)
    return prev * w_ref[0:1, :] + ext[off:off + rb] * w_ref[1:2, :] + nxt * w_ref[2:3, :]


def _rope(a, cos, sin):
    lane = lax.broadcasted_iota(jnp.int32, a.shape, 1)
    swapped = jnp.where((lane & 32) == 0, pltpu.roll(a, 96, axis=1), pltpu.roll(a, 32, axis=1))
    return a * cos + swapped * sin


def _inproj_kernel(x_ref, sh_ref, sc_ref, w_ref, ws_ref, qn_ref, kn_ref, cw_ref, cos_ref, sin_ref,
                   o_ref, os_ref, hx_ref, *, tile0, ntiles, seq_len, rope, rb):
    j = pl.program_id(1)
    tm = hx_ref.shape[0]
    assert tm % rb == 0

    @pl.when(j == 0)
    def _():
        x = x_ref[...]
        r = lax.rsqrt(jnp.mean(x * x, axis=-1, keepdims=True) + NORM_EPS)
        h = (x * r) * (1.0 + sc_ref[0]) + sh_ref[0]
        hb = h.astype(BF16)
        hx_ref[...] = hb
        os_ref[...] = _dot(hb, ws_ref[...])

    kind = j + tile0
    nh = PROJ_TILE // LANES

    def head(a, hh):
        return a[:, hh * LANES:(hh + 1) * LANES]

    def store_head(r0, hh, a):
        o_ref[r0:r0 + rb, hh * LANES:(hh + 1) * LANES] = a.astype(BF16)

    def qk_norm_rope(a, w, scale, r0):
        a = a * lax.rsqrt(jnp.mean(a * a, axis=-1, keepdims=True) + NORM_EPS) * w
        if rope:
            a = _rope(a, cos_ref[r0:r0 + rb, :], sin_ref[r0:r0 + rb, :])
        return a * scale if scale != 1.0 else a

    def ep_gates(acc, r0, lo):
        o_ref[r0:r0 + rb, :] = _sigmoid(acc).astype(BF16)

    def ep_attn_q(acc, r0, lo):
        for hh in range(nh):
            store_head(r0, hh, qk_norm_rope(head(acc, hh), qn_ref[...], HEAD_DIM ** -0.5, r0))

    def ep_plain(acc, r0, lo):
        o_ref[r0:r0 + rb, :] = acc.astype(BF16)

    def ep_gdn_qk(scale):
        def ep(ext, r0, lo):
            y = _silu(_conv3_block(ext, cw_ref, r0, lo, seq_len, rb))
            for hh in range(nh):
                a = head(y, hh)
                store_head(r0, hh, a * (lax.rsqrt(jnp.sum(a * a, axis=-1, keepdims=True) + NORM_EPS) * scale))
        return ep

    def ep_gdn_v(ext, r0, lo):
        o_ref[r0:r0 + rb, :] = _silu(_conv3_block(ext, cw_ref, r0, lo, seq_len, rb)).astype(BF16)

    def ep_attn_kv(acc, r0, lo):
        for hh in range(ATTN_KV_HEADS):
            store_head(r0, hh, qk_norm_rope(head(acc, hh), kn_ref[...], 1.0, r0))
        o_ref[r0:r0 + rb, ATTN_KV_HEADS * LANES:] = acc[:, ATTN_KV_HEADS * LANES:].astype(BF16)

    groups = [(0, 4, ep_gates, False), (4, 6, ep_attn_q, False), (6, 8, ep_plain, False),
              (8, 10, ep_gdn_qk(GDN_DK ** -0.5), True), (10, 12, ep_gdn_qk(1.0), True), (12, 14, ep_gdn_v, True),
              (14, 15, ep_attn_kv, False)]

    for k_lo, k_hi, epilogue, conv in groups:
        if max(k_lo, tile0) >= min(k_hi, tile0 + ntiles):
            continue

        def run(epilogue=epilogue, conv=conv):
            def rows(r):
                return _block_rows(r, rb, tm, seq_len) if conv else (r * rb, r * rb, (r + 1) * rb)

            def produce(r):
                _, lo, hi = rows(r)
                return _dot(hx_ref[lo:hi, :], w_ref[...])

            def consume(r, acc):
                r0, lo, _ = rows(r)
                epilogue(acc, r0, lo)

            _skewed(tm // rb, produce, consume)

        pl.when((kind >= k_lo) & (kind < k_hi))(run)


def _input_projection(x2, shift, scale, w_main, w_small, qn, kn, cw, cos, sin, *, tm, seq_len, tile0, ntiles, rope):
    m, d = x2.shape
    nrow = m // tm
    per_row_mod = shift.shape[0] > 1
    mod_map = (lambda i, j: (i, 0, 0)) if per_row_mod else (lambda i, j: (0, 0, 0))
    conv_tile0 = COL_GQ // PROJ_TILE
    n_conv_tiles = cw.shape[1] // PROJ_TILE
    kern = functools.partial(_inproj_kernel, tile0=tile0, ntiles=ntiles, seq_len=seq_len, rope=rope,
                             rb=min(2 * ROW_BLOCK, seq_len))
    return pl.pallas_call(
        kern,
        grid=(nrow, ntiles),
        in_specs=[
            pl.BlockSpec((tm, d), lambda i, j: (i, 0)),
            pl.BlockSpec((1, 1, d), mod_map),
            pl.BlockSpec((1, 1, d), mod_map),
            pl.BlockSpec((d, PROJ_TILE), lambda i, j: (0, j + tile0)),
            pl.BlockSpec((d, LANES), lambda i, j: (0, 0)),
            pl.BlockSpec((1, LANES), lambda i, j: (0, 0)),
            pl.BlockSpec((1, LANES), lambda i, j: (0, 0)),
            pl.BlockSpec((3, PROJ_TILE), lambda i, j: (0, jnp.clip(j + tile0 - conv_tile0, 0, n_conv_tiles - 1))),
            pl.BlockSpec(cos.shape, lambda i, j: (0, 0)),
            pl.BlockSpec(sin.shape, lambda i, j: (0, 0)),
        ],
        out_specs=[pl.BlockSpec((tm, PROJ_TILE), lambda i, j: (i, j)),
                   pl.BlockSpec((tm, LANES), lambda i, j: (i, 0))],
        out_shape=[jax.ShapeDtypeStruct((m, ntiles * PROJ_TILE), BF16),
                   jax.ShapeDtypeStruct((m, LANES), F32)],
        scratch_shapes=[pltpu.VMEM((tm, d), BF16)],
        compiler_params=pltpu.CompilerParams(dimension_semantics=("arbitrary", "arbitrary"),
                                             vmem_limit_bytes=VMEM_LIMIT),
    )(x2, shift, scale, w_main, w_small, qn, kn, cw, cos, sin)


def _gate_kernel(g_ref, alog_ref, dtb_ref, o_ref, *, nrows):
    db = g_ref[0, :nrows, :]
    da = g_ref[0, nrows:, :]
    beta = _sigmoid(db)
    z = da + dtb_ref[...]
    softplus = jnp.maximum(z, 0.0) + jnp.log(1.0 + jnp.exp(-jnp.abs(z)))
    la = -jnp.exp(alog_ref[...]) * softplus
    half = nrows // 2
    lane = lax.broadcasted_iota(jnp.int32, (half, LANES), 1)
    pre = la[:half]
    suf = la[half:]
    k = 1
    while k < LANES:
        pre = pre + jnp.where(lane >= k, pltpu.roll(pre, k, axis=1), 0.0)
        suf = suf + jnp.where(lane < LANES - k, pltpu.roll(suf, LANES - k, axis=1), 0.0)
        k *= 2
    gam = jnp.concatenate([pre, suf], axis=0)
    tot = jnp.sum(la, axis=-1, keepdims=True)
    eg = jnp.exp(gam)
    o_ref[0, 0] = beta
    o_ref[0, 1] = gam
    o_ref[0, 2] = eg
    o_ref[0, 3] = beta * eg
    o_ref[0, 4] = jnp.exp(tot - gam)
    o_ref[0, 5] = jnp.broadcast_to(jnp.exp(tot), gam.shape)


def _gates(g, alog_rows, dtb_rows):
    b, r2, _ = g.shape
    nrows = r2 // 2
    return pl.pallas_call(
        functools.partial(_gate_kernel, nrows=nrows),
        grid=(b,),
        in_specs=[pl.BlockSpec((1, r2, LANES), lambda i: (i, 0, 0)),
                  pl.BlockSpec((nrows, 1), lambda i: (0, 0)),
                  pl.BlockSpec((nrows, 1), lambda i: (0, 0))],
        out_specs=pl.BlockSpec((1, 6, nrows, LANES), lambda i: (i, 0, 0, 0)),
        out_shape=jax.ShapeDtypeStruct((b, 6, nrows, LANES), F32),
    )(g, alog_rows, dtb_rows)


def _attn_kernel(q_ref, kc_ref, vc_ref, kx_ref, vx_ref, o_ref):
    kc = kc_ref[...]
    kx = kx_ref[...]
    vc = vc_ref[...]
    vx = vx_ref[...]
    tq = q_ref.shape[0]
    rq = min(tq, 2 * ROW_BLOCK)
    nblk = tq // rq

    def unit(u):
        hh, blk = divmod(u, nblk)
        return slice(blk * rq, (blk + 1) * rq), slice(hh * LANES, (hh + 1) * LANES)

    def scores(u):
        rows, cols = unit(u)
        q = q_ref[rows, cols]
        return _dot_nt(q, kc), _dot_nt(q, kx)

    def softmax_pv(u, s):
        rows, cols = unit(u)
        sc, sx = s
        m = jnp.maximum(jnp.max(sc, axis=-1, keepdims=True), jnp.max(sx, axis=-1, keepdims=True))
        pc = jnp.exp(sc - m)
        px = jnp.exp(sx - m)
        denom = jnp.sum(pc, axis=-1, keepdims=True) + jnp.sum(px, axis=-1, keepdims=True)
        o = _dot(pc.astype(BF16), vc) + _dot(px.astype(BF16), vx)
        o_ref[rows, cols] = (o / denom).astype(BF16)

    _skewed(ATTN_GROUP * nblk, scores, softmax_pv)


def _attention(p_lat, p_ctx, *, b, n, cl, tq):
    gw = ATTN_GROUP * LANES
    nq = n // tq
    q0 = COL_AQ // gw
    kx0, vx0 = COL_AK // LANES, COL_AV // LANES
    kc0, vc0 = (COL_AK - CTX_COL0) // LANES, (COL_AV - CTX_COL0) // LANES
    return pl.pallas_call(
        _attn_kernel,
        grid=(b, ATTN_KV_HEADS, nq),
        in_specs=[
            pl.BlockSpec((tq, gw), lambda i, h, t: (i * nq + t, q0 + h)),
            pl.BlockSpec((cl, LANES), lambda i, h, t: (i, kc0 + h)),
            pl.BlockSpec((cl, LANES), lambda i, h, t: (i, vc0 + h)),
            pl.BlockSpec((n, LANES), lambda i, h, t: (i, kx0 + h)),
            pl.BlockSpec((n, LANES), lambda i, h, t: (i, vx0 + h)),
        ],
        out_specs=pl.BlockSpec((tq, gw), lambda i, h, t: (i * nq + t, h)),
        out_shape=jax.ShapeDtypeStruct((b * n, ATTN_HEADS * LANES), BF16),
        compiler_params=pltpu.CompilerParams(dimension_semantics=("arbitrary", "arbitrary", "arbitrary"),
                                             vmem_limit_bytes=VMEM_LIMIT),
    )(p_lat, p_ctx, p_ctx, p_lat, p_lat)


G_BETA, G_GAM, G_EG, G_BEG, G_EGD, G_GL = range(6)


def _gdn_kernel(ql_ref, kl_ref, vl_ref, kc_ref, vc_ref, g_ref, of_ref, ob_ref, mq_ref, bo_ref, s_ref,
                *, hb, ncc, ncl, cu):
    c_sz = GDN_CHUNK
    ri = lax.broadcasted_iota(jnp.int32, (c_sz, c_sz), 0)
    ci = lax.broadcasted_iota(jnp.int32, (c_sz, c_sz), 1)
    eye = (ri == ci).astype(F32)
    levels = int(math.log2(c_sz))
    cu_ctx = math.gcd(ncc, cu)
    cu_lat = math.gcd(ncl, cu)

    def tri_masks(d):
        lo, hi = (ci, ri) if d == 0 else (ri, ci)
        incl = hi >= lo
        strict = hi > lo
        pair = []
        for lv in range(levels):
            bh = hi >> lv
            bl = lo >> lv
            pair.append(((bh & 1) == 1) & (bl == bh - 1))
        return incl, strict, pair

    masks = [tri_masks(0), tri_masks(1)]

    def grow(gt, qi, d, hh):
        r = (qi * 2 + d) * hb + hh
        return gt[r:r + 1, :]

    def prep(chunks):
        inst = []
        for c, k, v, q in chunks:
            rq = pl.multiple_of(c * (2 * c_sz), 2 * c_sz)
            gt = g_ref[0, 0, c]
            gam_rows = gt[G_GAM * 2 * hb:(G_GAM + 1) * 2 * hb, :]
            gam_cols = jnp.concatenate([gam_rows, jnp.zeros((c_sz - 2 * hb, c_sz), F32)], axis=0).T
            for hh in range(hb):
                kh = k[hh]
                kk = _dot_nt(kh, kh)
                qk = None if q is None else _dot_nt(q[hh], kh)
                kt = kh.astype(F32).T
                for d in range(2):
                    inst.append(dict(rq=rq, gt=gt, hh=hh, d=d, hd=hh * 2 + d, kh=kh, vh=v[hh], kk=kk, qk=qk, kt=kt,
                                     qh=None if q is None else q[hh],
                                     gam_c=gam_cols[:, d * hb + hh:d * hb + hh + 1]))
        for it in inst:
            d, hh, gt = it["d"], it["hh"], it["gt"]
            incl, strict, pair = masks[d]
            beta_r = grow(gt, G_BETA, d, hh)
            e = jnp.exp(jnp.where(incl, it["gam_c"] - grow(gt, G_GAM, d, hh), NEG_BIG))
            l2 = jnp.where(strict, e * it["kk"], 0.0) * beta_r
            it["t"] = eye - jnp.where(pair[0], l2, 0.0)
            it["ms"] = [jnp.where(pair[lv], l2, 0.0).astype(BF16) for lv in range(1, levels)]
            it["kb"] = (it["kt"] * (grow(gt, G_EGD, d, hh) * beta_r)).astype(BF16)
            if it["qk"] is not None:
                pb = (e * it["qk"] * beta_r).astype(BF16)
                it["pb"] = pb
                it["qlhs"] = jnp.concatenate([(eye * grow(gt, G_EG, d, hh)).astype(BF16), -pb], axis=1)
        for lv in range(levels - 1):
            xs = [_dot(it["ms"][lv], it["t"].astype(BF16)).astype(BF16) for it in inst]
            for it, xm in zip(inst, xs):
                it["t"] = it["t"] - _dot(it["t"].astype(BF16), xm)
        for it in inst:
            it["yu"] = _dot(it["t"].astype(BF16), it["vh"]).astype(BF16)
        for it in inst:
            it["yw"] = _dot((it["t"] * grow(it["gt"], G_EG, it["d"], it["hh"])).astype(BF16), it["kh"]).astype(BF16)
        for it in inst:
            bm = _dot(it["kb"], jnp.concatenate([it["yu"], it["yw"]], axis=1))
            mq_ref[it["hd"], pl.ds(it["rq"], c_sz), :] = (-bm[:, GDN_DV:]).astype(BF16)
            bo_ref[it["hd"], pl.ds(it["rq"], c_sz), :] = bm[:, :GDN_DV].astype(BF16)
        for it in inst:
            if it["qk"] is not None:
                oo = _dot(it["pb"], it["yu"])
                qq = _dot(it["qlhs"], jnp.concatenate([it["qh"], it["yw"]], axis=0))
                mq_ref[it["hd"], pl.ds(it["rq"] + c_sz, c_sz), :] = qq.astype(BF16)
                bo_ref[it["hd"], pl.ds(it["rq"] + c_sz, c_sz), :] = oo.astype(BF16)

    def split_heads(ref, r0):
        return [ref[pl.ds(r0, c_sz), hh * LANES:(hh + 1) * LANES] for hh in range(hb)]

    def prep_ctx(i, carry):
        chunks = []
        for j in range(cu_ctx):
            c = i * cu_ctx + j
            r0 = pl.multiple_of(c * c_sz, c_sz)
            chunks.append((c, split_heads(kc_ref, r0), split_heads(vc_ref, r0), None))
        prep(chunks)
        return carry

    def prep_lat(i, carry):
        chunks = []
        for j in range(cu_lat):
            c = i * cu_lat + j
            r0 = pl.multiple_of(c * c_sz, c_sz)
            chunks.append((c + ncc, split_heads(kl_ref, r0), split_heads(vl_ref, r0), split_heads(ql_ref, r0)))
        prep(chunks)
        return carry

    lax.fori_loop(0, ncc // cu_ctx, prep_ctx, 0)
    lax.fori_loop(0, ncl // cu_lat, prep_lat, 0)

    s_ref[...] = jnp.zeros_like(s_ref)

    def scan_steps(steps):
        st = []
        for hh, d, c, out_row in steps:
            hd = hh * 2 + d
            rq = pl.multiple_of(c * (2 * c_sz), 2 * c_sz)
            rows = c_sz if out_row is None else 2 * c_sz
            s = s_ref[hd]
            st.append(dict(hd=hd, hh=hh, d=d, s=s, rq=rq, rows=rows, out_row=out_row,
                           gl=grow(g_ref[0, 0, c], G_GL, d, hh),
                           x=_dot(mq_ref[hd, pl.ds(rq, rows), :], s.astype(BF16))))
        for it in st:
            bo = bo_ref[it["hd"], pl.ds(it["rq"], it["rows"]), :].astype(F32)
            s_ref[it["hd"]] = it["gl"] * it["s"] + it["x"][:c_sz] + bo[:c_sz]
            if it["out_row"] is not None:
                out_ref = of_ref if it["d"] == 0 else ob_ref
                out_ref[pl.ds(it["out_row"], c_sz), it["hh"] * LANES:(it["hh"] + 1) * LANES] = (
                    it["x"][c_sz:] + bo[c_sz:]).astype(BF16)

    def scan_ctx(i, carry):
        steps = []
        for hh in range(hb):
            steps += [(hh, 0, i, None), (hh, 1, ncc - 1 - i, None)]
        scan_steps(steps)
        return carry

    def scan_lat(i, carry):
        steps = []
        for hh in range(hb):
            steps += [(hh, 0, ncc + i, pl.multiple_of(i * c_sz, c_sz)),
                      (hh, 1, ncc + ncl - 1 - i, pl.multiple_of((ncl - 1 - i) * c_sz, c_sz))]
        scan_steps(steps)
        return carry

    lax.fori_loop(0, ncc, scan_ctx, 0)
    lax.fori_loop(0, ncl, scan_lat, 0)


def _gdn(p_lat, p_ctx, gpack, *, b, n, cl, hb, cu):
    gw = hb * LANES
    ng = GDN_HEADS // hb
    ncc, ncl = cl // GDN_CHUNK, n // GDN_CHUNK
    nc = ncc + ncl
    q0, k0, v0 = COL_GQ // gw, COL_GK // gw, COL_GV // gw
    kc0, vc0 = (COL_GK - CTX_COL0) // gw, (COL_GV - CTX_COL0) // gw
    kern = functools.partial(_gdn_kernel, hb=hb, ncc=ncc, ncl=ncl, cu=cu)
    out = jax.ShapeDtypeStruct((b * n, GDN_HEADS * LANES), BF16)
    return pl.pallas_call(
        kern,
        grid=(b, ng),
        in_specs=[
            pl.BlockSpec((n, gw), lambda i, g: (i, q0 + g)),
            pl.BlockSpec((n, gw), lambda i, g: (i, k0 + g)),
            pl.BlockSpec((n, gw), lambda i, g: (i, v0 + g)),
            pl.BlockSpec((cl, gw), lambda i, g: (i, kc0 + g)),
            pl.BlockSpec((cl, gw), lambda i, g: (i, vc0 + g)),
            pl.BlockSpec((1, 1) + gpack.shape[2:], lambda i, g: (i, g, 0, 0, 0)),
        ],
        out_specs=[pl.BlockSpec((n, gw), lambda i, g: (i, g)), pl.BlockSpec((n, gw), lambda i, g: (i, g))],
        out_shape=[out, out],
        scratch_shapes=[
            pltpu.VMEM((2 * hb, 2 * nc * GDN_CHUNK, LANES), BF16),
            pltpu.VMEM((2 * hb, 2 * nc * GDN_CHUNK, LANES), BF16),
            pltpu.VMEM((2 * hb, GDN_DK, GDN_DV), F32),
        ],
        compiler_params=pltpu.CompilerParams(dimension_semantics=("arbitrary", "arbitrary"),
                                             vmem_limit_bytes=VMEM_LIMIT),
    )(p_lat, p_lat, p_lat, p_ctx, p_ctx, gpack)


def _merge_kernel(x_ref, a_ref, of_ref, ob_ref, z_ref, ga_ref, gd_ref, g1_ref, nw_ref, wpa_ref, wpd_ref, wout_ref,
                  o_ref):
    ya = _dot(a_ref[...], wpa_ref[...])
    heads = []
    for hh in range(GDN_HEADS):
        cols = slice(hh * LANES, (hh + 1) * LANES)
        o = of_ref[:, cols].astype(F32) + ob_ref[:, cols].astype(F32)
        o = o * lax.rsqrt(jnp.mean(o * o, axis=-1, keepdims=True) + NORM_EPS) * nw_ref[...]
        heads.append((o * _silu(z_ref[:, cols].astype(F32))).astype(BF16))
    yd = _dot(jnp.concatenate(heads, axis=1), wpd_ref[...])
    y = ga_ref[...].astype(F32) * ya + gd_ref[...].astype(F32) * yd
    o_ref[...] = x_ref[...] + g1_ref[0] * _dot(y.astype(BF16), wout_ref[...])


def _merge(x2, attn, o_f, o_b, p_lat, g1, nw, wpa, wpd, wout, *, n, tm):
    m, d = x2.shape
    per_seq = n // tm
    row = lambda i: (i, 0)
    const = lambda i: (0, 0)
    return pl.pallas_call(
        _merge_kernel,
        grid=(m // tm,),
        in_specs=[
            pl.BlockSpec((tm, d), row),
            pl.BlockSpec((tm, d), row),
            pl.BlockSpec((tm, d), row),
            pl.BlockSpec((tm, d), row),
            pl.BlockSpec((tm, d), lambda i: (i, COL_Z // d)),
            pl.BlockSpec((tm, d), lambda i: (i, COL_GATES // d)),
            pl.BlockSpec((tm, d), lambda i: (i, COL_GATES // d + 1)),
            pl.BlockSpec((1, 1, d), lambda i: (i // per_seq, 0, 0)),
            pl.BlockSpec((1, LANES), const),
            pl.BlockSpec((d, d), const),
            pl.BlockSpec((d, d), const),
            pl.BlockSpec((d, d), const),
        ],
        out_specs=pl.BlockSpec((tm, d), row),
        out_shape=jax.ShapeDtypeStruct((m, d), F32),
        compiler_params=pltpu.CompilerParams(dimension_semantics=("arbitrary",), vmem_limit_bytes=VMEM_LIMIT),
    )(x2, attn, o_f, o_b, p_lat, p_lat, p_lat, g1, nw, wpa, wpd, wout)


def _ffn_kernel(x_ref, sh_ref, sc_ref, g2_ref, wg_ref, wv_ref, cwg_ref, cwv_ref, cbg_ref, cbv_ref, wd_ref, fw_ref,
                o_ref, h_ref, *, seq_len, rb):
    j = pl.program_id(1)
    tm = h_ref.shape[0]
    assert tm % rb == 0

    @pl.when(j == 0)
    def _():
        x = x_ref[...]
        r = lax.rsqrt(jnp.mean(x * x, axis=-1, keepdims=True) + NORM_EPS)
        h_ref[...] = ((x * r) * (1.0 + sc_ref[0]) + sh_ref[0]).astype(BF16)
        o_ref[...] = jnp.zeros_like(o_ref)

    def produce(r):
        _, lo, hi = _block_rows(r, rb, tm, seq_len)
        h = h_ref[lo:hi, :]
        return _dot(h, wg_ref[...]), _dot(h, wv_ref[...])

    def consume(r, up):
        r0, lo, _ = _block_rows(r, rb, tm, seq_len)
        ug = _conv3_block(up[0], cwg_ref, r0, lo, seq_len, rb) + cbg_ref[...]
        uv = _conv3_block(up[1], cwv_ref, r0, lo, seq_len, rb) + cbv_ref[...]
        o_ref[r0:r0 + rb, :] += _dot((_silu(ug) * uv).astype(BF16), wd_ref[...])

    _skewed(tm // rb, produce, consume)

    @pl.when(j == pl.num_programs(1) - 1)
    def _():
        y = x_ref[...] + g2_ref[0] * o_ref[...]
        o_ref[...] = y * lax.rsqrt(jnp.mean(y * y, axis=-1, keepdims=True) + NORM_EPS) * fw_ref[...]


def _ffn(x1, sh2, sc2, g2, wup, cw, cb, wdown, fw, *, n, tf, rb):
    m, d = x1.shape
    dff = wdown.shape[0]
    nj = dff // tf
    mod_map = lambda i, j: (i, 0, 0)
    return pl.pallas_call(
        functools.partial(_ffn_kernel, seq_len=n, rb=min(rb, n)),
        grid=(m // n, nj),
        in_specs=[
            pl.BlockSpec((n, d), lambda i, j: (i, 0)),
            pl.BlockSpec((1, 1, d), mod_map),
            pl.BlockSpec((1, 1, d), mod_map),
            pl.BlockSpec((1, 1, d), mod_map),
            pl.BlockSpec((d, tf), lambda i, j: (0, j)),
            pl.BlockSpec((d, tf), lambda i, j: (0, j + nj)),
            pl.BlockSpec((3, tf), lambda i, j: (0, j)),
            pl.BlockSpec((3, tf), lambda i, j: (0, j + nj)),
            pl.BlockSpec((1, tf), lambda i, j: (0, j)),
            pl.BlockSpec((1, tf), lambda i, j: (0, j + nj)),
            pl.BlockSpec((tf, d), lambda i, j: (j, 0)),
            pl.BlockSpec((1, d), lambda i, j: (0, 0)),
        ],
        out_specs=pl.BlockSpec((n, d), lambda i, j: (i, 0)),
        out_shape=jax.ShapeDtypeStruct((m, d), F32),
        scratch_shapes=[pltpu.VMEM((n, d), BF16)],
        compiler_params=pltpu.CompilerParams(dimension_semantics=("arbitrary", "arbitrary"),
                                             vmem_limit_bytes=VMEM_LIMIT),
    )(x1, sh2, sc2, g2, wup, wup, cw, cw, cb, cb, wdown, fw)


def _rope_tables(n):
    rows = n // GRID_W
    row_ids = jnp.broadcast_to(jnp.arange(rows, dtype=F32)[:, None], (rows, GRID_W)).reshape(n)
    col_ids = jnp.broadcast_to(jnp.arange(GRID_W, dtype=F32)[None, :], (rows, GRID_W)).reshape(n)
    inv_freq = ROPE_THETA ** (-jnp.arange(0, ROPE_AXIS_DIM, 2, dtype=F32) / ROPE_AXIS_DIM)
    ar = row_ids[:, None] * inv_freq
    ac = col_ids[:, None] * inv_freq
    cos = jnp.concatenate([jnp.cos(ar), jnp.cos(ar), jnp.cos(ac), jnp.cos(ac)], axis=-1)
    sin = jnp.concatenate([-jnp.sin(ar), jnp.sin(ar), -jnp.sin(ac), jnp.sin(ac)], axis=-1)
    return cos, sin


def kernel(x, c, ctx, c_ctx, w_mod, b_mod, w_in, q_norm_w, k_norm_w, conv_qkv_w, a_log, dt_bias, gdn_norm_w, w_pa,
           w_pd, w_out, w_up, ffn_conv_w, ffn_conv_b, w_down, final_norm_w):
    assert w_mod.shape[0] == 1, "single-layer block"
    b, n, d = x.shape
    cl = ctx.shape[1]
    hb = 4

    pad = (-(b + 1)) % 8
    cc = jnp.concatenate([c, c_ctx[None, :], jnp.zeros((pad, d), F32)], axis=0)
    mod = _modulation(cc, w_mod[0], b_mod[0][None, :])
    sh1, sc1, g1, sh2, sc2, g2 = [t[:, None, :] for t in jnp.split(mod[:b], 6, axis=-1)]
    csh1, csc1 = mod[b:b + 1, None, :d], mod[b:b + 1, None, d:2 * d]

    w = w_in[0]
    akv, qkv_w = ATTN_KV_HEADS * HEAD_DIM, 3 * GDN_HEADS * GDN_DK
    o_qkv = 2 * akv
    o_db = o_qkv + qkv_w
    o_aq = o_db + 4 * GDN_HEADS
    o_z = o_aq + ATTN_HEADS * HEAD_DIM
    o_gate = o_z + GDN_HEADS * GDN_DV
    w_main = jnp.concatenate([w[:, o_gate:], w[:, o_aq:o_z], w[:, o_z:o_gate], w[:, o_qkv:o_db], w[:, :o_qkv]],
                             axis=1).astype(BF16)
    w_small = jnp.concatenate([w[:, o_db:o_aq], jnp.zeros((d, LANES - 4 * GDN_HEADS), F32)], axis=1).astype(BF16)
    cos, sin = _rope_tables(n)
    x2 = x.reshape(b * n, d)
    common = (w_main, w_small, q_norm_w, k_norm_w, conv_qkv_w[0], cos, sin)
    p_lat, s_lat = _input_projection(x2, sh1, sc1, *common, tm=n, seq_len=n, tile0=0,
                                     ntiles=MAIN_COLS // PROJ_TILE, rope=True)
    p_ctx, s_ctx = _input_projection(ctx.reshape(b * cl, d), csh1, csc1, *common, tm=b * cl, seq_len=cl,
                                     tile0=CTX_COL0 // PROJ_TILE, ntiles=(MAIN_COLS - CTX_COL0) // PROJ_TILE,
                                     rope=False)

    ncc, ncl = cl // GDN_CHUNK, n // GDN_CHUNK
    nc = ncc + ncl
    ng = GDN_HEADS // hb
    nsm = 4 * GDN_HEADS
    small = jnp.concatenate([s_ctx[:, :nsm].reshape(b, cl, nsm), s_lat[:, :nsm].reshape(b, n, nsm)], axis=1)
    small = small.transpose(0, 2, 1).reshape(b, nsm * nc, GDN_CHUNK)
    alog_rows = jnp.repeat(a_log[0].reshape(-1), nc)[:, None]
    dtb_rows = jnp.repeat(dt_bias[0].reshape(-1), nc)[:, None]
    gq = _gates(small, alog_rows, dtb_rows).reshape(b, 6, 2, ng, hb, nc, GDN_CHUNK)
    gpack = gq.transpose(0, 3, 5, 1, 2, 4, 6).reshape(b, ng, nc, 6 * 2 * hb, GDN_CHUNK)

    attn = _attention(p_lat, p_ctx, b=b, n=n, cl=cl, tq=min(n, 1024))
    o_f, o_b = _gdn(p_lat, p_ctx, gpack, b=b, n=n, cl=cl, hb=hb, cu=2)

    x1 = _merge(x2, attn, o_f, o_b, p_lat, g1, gdn_norm_w, w_pa[0].astype(BF16), w_pd[0].astype(BF16),
                w_out[0].astype(BF16), n=n, tm=min(n, 512))

    out = _ffn(x1, sh2, sc2, g2, w_up[0].astype(BF16), ffn_conv_w[0], ffn_conv_b[0][None, :],
               w_down[0].astype(BF16), final_norm_w[None, :], n=n, tf=256, rb=4 * ROW_BLOCK)
    return out.reshape(b, n, d)
```

```python
import functools
import math

import jax
import jax.numpy as jnp
from jax import lax
from jax.experimental import pallas as pl
from jax.experimental.pallas import tpu as pltpu

F32 = jnp.float32
BF16 = jnp.bfloat16

LANES = 128
GRID_W = 64
ATTN_HEADS = 8
ATTN_KV_HEADS = 2
ATTN_GROUP = ATTN_HEADS // ATTN_KV_HEADS
HEAD_DIM = 128
ROPE_AXIS_DIM = HEAD_DIM // 2
ROPE_THETA = 10000.0
GDN_HEADS = 8
GDN_DK = 128
GDN_DV = 128
GDN_CHUNK = 128
NORM_EPS = 1e-6
NEG_BIG = -1e30

PROJ_TILE = 512
COL_GATES = 0
COL_AQ = 2048
COL_Z = 3072
COL_GQ = 4096
COL_GK = 5120
COL_GV = 6144
COL_AK = 7168
COL_AV = 7424
MAIN_COLS = 7680
CTX_COL0 = COL_GQ
VMEM_LIMIT = 56 * 1024 * 1024


def _dot(a, b):
    return jnp.dot(a, b, preferred_element_type=F32)


def _dot_nt(a, b):
    return lax.dot_general(a, b, (((1,), (1,)), ((), ())), preferred_element_type=F32)


def _sigmoid(x):
    return 0.5 * jnp.tanh(0.5 * x) + 0.5


def _silu(x):
    h = 0.5 * x
    return h * jnp.tanh(h) + h


def _mod_kernel(c_ref, w_ref, b_ref, o_ref):
    s = _silu(c_ref[...])
    w = w_ref[...]
    s_hi = s.astype(BF16)
    s_lo = (s - s_hi.astype(F32)).astype(BF16)
    w_hi = w.astype(BF16)
    w_lo = (w - w_hi.astype(F32)).astype(BF16)
    o_ref[...] = _dot(s_hi, w_hi) + _dot(s_lo, w_hi) + _dot(s_hi, w_lo) + b_ref[...]


def _modulation(cc, w_mod, b_mod):
    rows, d = cc.shape
    n = w_mod.shape[1]
    tn = 1024
    return pl.pallas_call(
        _mod_kernel,
        grid=(n // tn,),
        in_specs=[pl.BlockSpec((rows, d), lambda j: (0, 0)),
                  pl.BlockSpec((d, tn), lambda j: (0, j)),
                  pl.BlockSpec((1, tn), lambda j: (0, j))],
        out_specs=pl.BlockSpec((rows, tn), lambda j: (0, j)),
        out_shape=jax.ShapeDtypeStruct((rows, n), F32),
    )(cc, w_mod, b_mod)


ROW_BLOCK = 256
ROW_HALO = 16


def _skewed(n, produce, consume):
    prev = produce(0)
    for r in range(1, n):
        cur = produce(r)
        consume(r - 1, prev)
        prev = cur
    consume(n - 1, prev)


def _block_rows(r, rb, tm, seq_len):
    r0 = r * rb
    halo = ROW_HALO if seq_len > rb else 0
    return r0, max(r0 - halo, 0), min(r0 + rb + halo, tm)


def _conv3_block(ext, w_ref, r0, lo, seq_len, rb):
    assert seq_len % rb == 0
    n_ext = ext.shape[0]
    off = r0 - lo
    prev = pltpu.roll(ext, 1, axis=0)[off:off + rb]
    nxt = pltpu.roll(ext, n_ext - 1, axis=0)[off:off + rb]
    row = lax.broadcasted_iota(jnp.int32, (rb, 1), 0)
    if r0 % seq_len == 0:
        prev = jnp.where(row == 0, 0.0, prev)
    if (r0 + rb) % seq_len == 0:
        nxt = jnp.where(row == rb - 1, 0.0, nxt)
    return prev * w_ref[0:1, :] + ext[off:off + rb] * w_ref[1:2, :] + nxt * w_ref[2:3, :]


def _pair_major(t):
    lead = t.shape[:-1]
    t = t.reshape(lead + (t.shape[-1] // HEAD_DIM, 2, 2, ROPE_AXIS_DIM // 2))
    return jnp.swapaxes(t, -2, -3).reshape(lead + (-1,))


def _rope(a, cos, sin):
    return a * cos + pltpu.roll(a, HEAD_DIM // 2, axis=1) * sin


def _inproj_kernel(x_ref, sh_ref, sc_ref, w_ref, ws_ref, qn_ref, kn_ref, cw_ref, qc_ref, qs_ref, kc_ref, ks_ref,
                   o_ref, os_ref, hx_ref, *, tile0, ntiles, seq_len, rope, rb):
    j = pl.program_id(1)
    tm = hx_ref.shape[0]
    assert tm % rb == 0

    @pl.when(j == 0)
    def _():
        x = x_ref[...]
        r = lax.rsqrt(jnp.mean(x * x, axis=-1, keepdims=True) + NORM_EPS)
        h = (x * r) * (1.0 + sc_ref[0]) + sh_ref[0]
        hb = h.astype(BF16)
        hx_ref[...] = hb
        os_ref[...] = _dot(hb, ws_ref[...])

    kind = j + tile0
    nh = PROJ_TILE // LANES

    def head(a, hh):
        return a[:, hh * LANES:(hh + 1) * LANES]

    def store_head(r0, hh, a):
        o_ref[r0:r0 + rb, hh * LANES:(hh + 1) * LANES] = a.astype(BF16)

    def qk_norm_rope(a, w_ref, cos_ref, sin_ref, scale, r0):
        r = lax.rsqrt(jnp.mean(a * a, axis=-1, keepdims=True) + NORM_EPS)
        if rope:
            return r * _rope(a, cos_ref[r0:r0 + rb, :], sin_ref[r0:r0 + rb, :])
        return a * (r * scale) * w_ref[...]

    def ep_gates(acc, r0, lo):
        o_ref[r0:r0 + rb, :] = _sigmoid(acc).astype(BF16)

    def ep_attn_q(acc, r0, lo):
        for hh in range(nh):
            store_head(r0, hh, qk_norm_rope(head(acc, hh), qn_ref, qc_ref, qs_ref, HEAD_DIM ** -0.5, r0))

    def ep_plain(acc, r0, lo):
        o_ref[r0:r0 + rb, :] = acc.astype(BF16)

    def ep_gdn_qk(scale):
        def ep(ext, r0, lo):
            y = _silu(_conv3_block(ext, cw_ref, r0, lo, seq_len, rb))
            for hh in range(nh):
                a = head(y, hh)
                store_head(r0, hh, a * (lax.rsqrt(jnp.sum(a * a, axis=-1, keepdims=True) + NORM_EPS) * scale))
        return ep

    def ep_gdn_v(ext, r0, lo):
        o_ref[r0:r0 + rb, :] = _silu(_conv3_block(ext, cw_ref, r0, lo, seq_len, rb)).astype(BF16)

    def ep_attn_kv(acc, r0, lo):
        for hh in range(ATTN_KV_HEADS):
            store_head(r0, hh, qk_norm_rope(head(acc, hh), kn_ref, kc_ref, ks_ref, 1.0, r0))
        o_ref[r0:r0 + rb, ATTN_KV_HEADS * LANES:] = acc[:, ATTN_KV_HEADS * LANES:].astype(BF16)

    groups = [(0, 4, ep_gates, False), (4, 6, ep_attn_q, False), (6, 8, ep_plain, False),
              (8, 10, ep_gdn_qk(GDN_DK ** -0.5), True), (10, 12, ep_gdn_qk(1.0), True), (12, 14, ep_gdn_v, True),
              (14, 15, ep_attn_kv, False)]

    for k_lo, k_hi, epilogue, conv in groups:
        if max(k_lo, tile0) >= min(k_hi, tile0 + ntiles):
            continue

        def run(epilogue=epilogue, conv=conv):
            def rows(r):
                return _block_rows(r, rb, tm, seq_len) if conv else (r * rb, r * rb, (r + 1) * rb)

            def produce(r):
                _, lo, hi = rows(r)
                return _dot(hx_ref[lo:hi, :], w_ref[...])

            def consume(r, acc):
                r0, lo, _ = rows(r)
                epilogue(acc, r0, lo)

            _skewed(tm // rb, produce, consume)

        pl.when((kind >= k_lo) & (kind < k_hi))(run)


def _input_projection(x2, shift, scale, w_main, w_small, qn, kn, cw, tables, *, tm, seq_len, tile0, ntiles, rope):
    m, d = x2.shape
    nrow = m // tm
    per_row_mod = shift.shape[0] > 1
    mod_map = (lambda i, j: (i, 0, 0)) if per_row_mod else (lambda i, j: (0, 0, 0))
    conv_tile0 = COL_GQ // PROJ_TILE
    n_conv_tiles = cw.shape[1] // PROJ_TILE
    kern = functools.partial(_inproj_kernel, tile0=tile0, ntiles=ntiles, seq_len=seq_len, rope=rope,
                             rb=min(2 * ROW_BLOCK, seq_len))
    return pl.pallas_call(
        kern,
        grid=(nrow, ntiles),
        in_specs=[
            pl.BlockSpec((tm, d), lambda i, j: (i, 0)),
            pl.BlockSpec((1, 1, d), mod_map),
            pl.BlockSpec((1, 1, d), mod_map),
            pl.BlockSpec((d, PROJ_TILE), lambda i, j: (0, j + tile0)),
            pl.BlockSpec((d, LANES), lambda i, j: (0, 0)),
            pl.BlockSpec((1, LANES), lambda i, j: (0, 0)),
            pl.BlockSpec((1, LANES), lambda i, j: (0, 0)),
            pl.BlockSpec((3, PROJ_TILE), lambda i, j: (0, jnp.clip(j + tile0 - conv_tile0, 0, n_conv_tiles - 1))),
        ] + [pl.BlockSpec(t.shape, lambda i, j: (0, 0), pipeline_mode=pl.Buffered(1)) for t in tables],
        out_specs=[pl.BlockSpec((tm, PROJ_TILE), lambda i, j: (i, j)),
                   pl.BlockSpec((tm, LANES), lambda i, j: (i, 0))],
        out_shape=[jax.ShapeDtypeStruct((m, ntiles * PROJ_TILE), BF16),
                   jax.ShapeDtypeStruct((m, LANES), F32)],
        scratch_shapes=[pltpu.VMEM((tm, d), BF16)],
        compiler_params=pltpu.CompilerParams(dimension_semantics=("arbitrary", "arbitrary"),
                                             vmem_limit_bytes=VMEM_LIMIT),
    )(x2, shift, scale, w_main, w_small, qn, kn, cw, *tables)


def _gate_kernel(g_ref, alog_ref, dtb_ref, o_ref, *, nrows):
    db = g_ref[0, :nrows, :]
    da = g_ref[0, nrows:, :]
    beta = _sigmoid(db)
    z = da + dtb_ref[...]
    softplus = jnp.maximum(z, 0.0) + jnp.log(1.0 + jnp.exp(-jnp.abs(z)))
    la = -jnp.exp(alog_ref[...]) * softplus
    half = nrows // 2
    lane = lax.broadcasted_iota(jnp.int32, (half, LANES), 1)
    pre = la[:half]
    suf = la[half:]
    k = 1
    while k < LANES:
        pre = pre + jnp.where(lane >= k, pltpu.roll(pre, k, axis=1), 0.0)
        suf = suf + jnp.where(lane < LANES - k, pltpu.roll(suf, LANES - k, axis=1), 0.0)
        k *= 2
    gam = jnp.concatenate([pre, suf], axis=0)
    tot = jnp.sum(la, axis=-1, keepdims=True)
    eg = jnp.exp(gam)
    o_ref[0, 0] = beta
    o_ref[0, 1] = gam
    o_ref[0, 2] = eg
    o_ref[0, 3] = beta * eg
    o_ref[0, 4] = jnp.exp(tot - gam)
    o_ref[0, 5] = jnp.broadcast_to(jnp.exp(tot), gam.shape)


def _gates(g, alog_rows, dtb_rows):
    b, r2, _ = g.shape
    nrows = r2 // 2
    return pl.pallas_call(
        functools.partial(_gate_kernel, nrows=nrows),
        grid=(b,),
        in_specs=[pl.BlockSpec((1, r2, LANES), lambda i: (i, 0, 0)),
                  pl.BlockSpec((nrows, 1), lambda i: (0, 0)),
                  pl.BlockSpec((nrows, 1), lambda i: (0, 0))],
        out_specs=pl.BlockSpec((1, 6, nrows, LANES), lambda i: (i, 0, 0, 0)),
        out_shape=jax.ShapeDtypeStruct((b, 6, nrows, LANES), F32),
    )(g, alog_rows, dtb_rows)


def _attn_kernel(q_ref, kc_ref, vc_ref, kx_ref, vx_ref, o_ref):
    kc = kc_ref[...]
    kx = kx_ref[...]
    vc = vc_ref[...]
    vx = vx_ref[...]
    tq = q_ref.shape[0]
    rq = min(tq, 2 * ROW_BLOCK)
    nblk = tq // rq

    def unit(u):
        hh, blk = divmod(u, nblk)
        return slice(blk * rq, (blk + 1) * rq), slice(hh * LANES, (hh + 1) * LANES)

    def scores(u):
        rows, cols = unit(u)
        q = q_ref[rows, cols]
        return _dot_nt(q, kc), _dot_nt(q, kx)

    def softmax_pv(u, s):
        rows, cols = unit(u)
        sc, sx = s
        m = jnp.maximum(jnp.max(sc, axis=-1, keepdims=True), jnp.max(sx, axis=-1, keepdims=True))
        pc = jnp.exp(sc - m)
        px = jnp.exp(sx - m)
        denom = jnp.sum(pc, axis=-1, keepdims=True) + jnp.sum(px, axis=-1, keepdims=True)
        o = _dot(pc.astype(BF16), vc) + _dot(px.astype(BF16), vx)
        o_ref[rows, cols] = (o / denom).astype(BF16)

    _skewed(ATTN_GROUP * nblk, scores, softmax_pv)


def _attention(p_lat, p_ctx, *, b, n, cl, tq):
    gw = ATTN_GROUP * LANES
    nq = n // tq
    q0 = COL_AQ // gw
    kx0, vx0 = COL_AK // LANES, COL_AV // LANES
    kc0, vc0 = (COL_AK - CTX_COL0) // LANES, (COL_AV - CTX_COL0) // LANES
    return pl.pallas_call(
        _attn_kernel,
        grid=(b, ATTN_KV_HEADS, nq),
        in_specs=[
            pl.BlockSpec((tq, gw), lambda i, h, t: (i * nq + t, q0 + h)),
            pl.BlockSpec((cl, LANES), lambda i, h, t: (i, kc0 + h)),
            pl.BlockSpec((cl, LANES), lambda i, h, t: (i, vc0 + h)),
            pl.BlockSpec((n, LANES), lambda i, h, t: (i, kx0 + h)),
            pl.BlockSpec((n, LANES), lambda i, h, t: (i, vx0 + h)),
        ],
        out_specs=pl.BlockSpec((tq, gw), lambda i, h, t: (i * nq + t, h)),
        out_shape=jax.ShapeDtypeStruct((b * n, ATTN_HEADS * LANES), BF16),
        compiler_params=pltpu.CompilerParams(dimension_semantics=("arbitrary", "arbitrary", "arbitrary"),
                                             vmem_limit_bytes=VMEM_LIMIT),
    )(p_lat, p_ctx, p_ctx, p_lat, p_lat)


G_BETA, G_GAM, G_EG, G_BEG, G_EGD, G_GL = range(6)


def _gdn_kernel(ql_ref, kl_ref, vl_ref, kc_ref, vc_ref, g_ref, of_ref, ob_ref, mq_ref, bo_ref, s_ref,
                *, hb, ncc, ncl, cu):
    c_sz = GDN_CHUNK
    ri = lax.broadcasted_iota(jnp.int32, (c_sz, c_sz), 0)
    ci = lax.broadcasted_iota(jnp.int32, (c_sz, c_sz), 1)
    eye = (ri == ci).astype(F32)
    levels = int(math.log2(c_sz))
    cu_ctx = math.gcd(ncc, cu)
    cu_lat = math.gcd(ncl, cu)

    def tri_masks(d):
        lo, hi = (ci, ri) if d == 0 else (ri, ci)
        incl = hi >= lo
        strict = hi > lo
        pair = []
        for lv in range(levels):
            bh = hi >> lv
            bl = lo >> lv
            pair.append(((bh & 1) == 1) & (bl == bh - 1))
        return incl, strict, pair

    masks = [tri_masks(0), tri_masks(1)]

    def grow(gt, qi, d, hh):
        r = (qi * 2 + d) * hb + hh
        return gt[r:r + 1, :]

    def prep(chunks):
        inst = []
        for c, k, v, q in chunks:
            rq = pl.multiple_of(c * (2 * c_sz), 2 * c_sz)
            gt = g_ref[0, 0, c]
            gam_rows = gt[G_GAM * 2 * hb:(G_GAM + 1) * 2 * hb, :]
            gam_cols = jnp.concatenate([gam_rows, jnp.zeros((c_sz - 2 * hb, c_sz), F32)], axis=0).T
            for hh in range(hb):
                kh = k[hh]
                kk = _dot_nt(kh, kh)
                qk = None if q is None else _dot_nt(q[hh], kh)
                kt = kh.astype(F32).T
                for d in range(2):
                    inst.append(dict(rq=rq, gt=gt, hh=hh, d=d, hd=hh * 2 + d, kh=kh, vh=v[hh], kk=kk, qk=qk, kt=kt,
                                     qh=None if q is None else q[hh],
                                     gam_c=gam_cols[:, d * hb + hh:d * hb + hh + 1]))
        for it in inst:
            d, hh, gt = it["d"], it["hh"], it["gt"]
            incl, strict, pair = masks[d]
            beta_r = grow(gt, G_BETA, d, hh)
            e = jnp.exp(jnp.where(incl, it["gam_c"] - grow(gt, G_GAM, d, hh), NEG_BIG))
            l2 = jnp.where(strict, e * it["kk"], 0.0) * beta_r
            it["t"] = eye - jnp.where(pair[0], l2, 0.0)
            it["ms"] = [jnp.where(pair[lv], l2, 0.0).astype(BF16) for lv in range(1, levels)]
            it["kb"] = (it["kt"] * (grow(gt, G_EGD, d, hh) * beta_r)).astype(BF16)
            if it["qk"] is not None:
                pb = (e * it["qk"] * beta_r).astype(BF16)
                it["pb"] = pb
                it["qlhs"] = jnp.concatenate([(eye * grow(gt, G_EG, d, hh)).astype(BF16), -pb], axis=1)
        for lv in range(levels - 1):
            xs = [_dot(it["ms"][lv], it["t"].astype(BF16)).astype(BF16) for it in inst]
            for it, xm in zip(inst, xs):
                it["t"] = it["t"] - _dot(it["t"].astype(BF16), xm)
        for it in inst:
            it["yu"] = _dot(it["t"].astype(BF16), it["vh"]).astype(BF16)
        for it in inst:
            it["yw"] = _dot((it["t"] * grow(it["gt"], G_EG, it["d"], it["hh"])).astype(BF16), it["kh"]).astype(BF16)
        for it in inst:
            bm = _dot(it["kb"], jnp.concatenate([it["yu"], it["yw"]], axis=1))
            mq_ref[it["hd"], pl.ds(it["rq"], c_sz), :] = (-bm[:, GDN_DV:]).astype(BF16)
            bo_ref[it["hd"], pl.ds(it["rq"], c_sz), :] = bm[:, :GDN_DV].astype(BF16)
        for it in inst:
            if it["qk"] is not None:
                oo = _dot(it["pb"], it["yu"])
                qq = _dot(it["qlhs"], jnp.concatenate([it["qh"], it["yw"]], axis=0))
                mq_ref[it["hd"], pl.ds(it["rq"] + c_sz, c_sz), :] = qq.astype(BF16)
                bo_ref[it["hd"], pl.ds(it["rq"] + c_sz, c_sz), :] = oo.astype(BF16)

    def split_heads(ref, r0):
        return [ref[pl.ds(r0, c_sz), hh * LANES:(hh + 1) * LANES] for hh in range(hb)]

    def prep_ctx(i, carry):
        chunks = []
        for j in range(cu_ctx):
            c = i * cu_ctx + j
            r0 = pl.multiple_of(c * c_sz, c_sz)
            chunks.append((c, split_heads(kc_ref, r0), split_heads(vc_ref, r0), None))
        prep(chunks)
        return carry

    def prep_lat(i, carry):
        chunks = []
        for j in range(cu_lat):
            c = i * cu_lat + j
            r0 = pl.multiple_of(c * c_sz, c_sz)
            chunks.append((c + ncc, split_heads(kl_ref, r0), split_heads(vl_ref, r0), split_heads(ql_ref, r0)))
        prep(chunks)
        return carry

    lax.fori_loop(0, ncc // cu_ctx, prep_ctx, 0)
    lax.fori_loop(0, ncl // cu_lat, prep_lat, 0)

    s_ref[...] = jnp.zeros_like(s_ref)

    def scan_steps(steps):
        st = []
        for hh, d, c, out_row in steps:
            hd = hh * 2 + d
            rq = pl.multiple_of(c * (2 * c_sz), 2 * c_sz)
            rows = c_sz if out_row is None else 2 * c_sz
            s = s_ref[hd]
            st.append(dict(hd=hd, hh=hh, d=d, s=s, rq=rq, rows=rows, out_row=out_row,
                           gl=grow(g_ref[0, 0, c], G_GL, d, hh),
                           x=_dot(mq_ref[hd, pl.ds(rq, rows), :], s.astype(BF16))))
        for it in st:
            bo = bo_ref[it["hd"], pl.ds(it["rq"], it["rows"]), :].astype(F32)
            s_ref[it["hd"]] = it["gl"] * it["s"] + it["x"][:c_sz] + bo[:c_sz]
            if it["out_row"] is not None:
                out_ref = of_ref if it["d"] == 0 else ob_ref
                out_ref[pl.ds(it["out_row"], c_sz), it["hh"] * LANES:(it["hh"] + 1) * LANES] = (
                    it["x"][c_sz:] + bo[c_sz:]).astype(BF16)

    def scan_ctx(i, carry):
        steps = []
        for hh in range(hb):
            steps += [(hh, 0, i, None), (hh, 1, ncc - 1 - i, None)]
        scan_steps(steps)
        return carry

    def scan_lat(i, carry):
        steps = []
        for hh in range(hb):
            steps += [(hh, 0, ncc + i, pl.multiple_of(i * c_sz, c_sz)),
                      (hh, 1, ncc + ncl - 1 - i, pl.multiple_of((ncl - 1 - i) * c_sz, c_sz))]
        scan_steps(steps)
        return carry

    lax.fori_loop(0, ncc, scan_ctx, 0)
    lax.fori_loop(0, ncl, scan_lat, 0)


def _gdn(p_lat, p_ctx, gpack, *, b, n, cl, hb, cu):
    gw = hb * LANES
    ng = GDN_HEADS // hb
    ncc, ncl = cl // GDN_CHUNK, n // GDN_CHUNK
    nc = ncc + ncl
    q0, k0, v0 = COL_GQ // gw, COL_GK // gw, COL_GV // gw
    kc0, vc0 = (COL_GK - CTX_COL0) // gw, (COL_GV - CTX_COL0) // gw
    kern = functools.partial(_gdn_kernel, hb=hb, ncc=ncc, ncl=ncl, cu=cu)
    out = jax.ShapeDtypeStruct((b * n, GDN_HEADS * LANES), BF16)
    return pl.pallas_call(
        kern,
        grid=(b, ng),
        in_specs=[
            pl.BlockSpec((n, gw), lambda i, g: (i, q0 + g)),
            pl.BlockSpec((n, gw), lambda i, g: (i, k0 + g)),
            pl.BlockSpec((n, gw), lambda i, g: (i, v0 + g)),
            pl.BlockSpec((cl, gw), lambda i, g: (i, kc0 + g)),
            pl.BlockSpec((cl, gw), lambda i, g: (i, vc0 + g)),
            pl.BlockSpec((1, 1) + gpack.shape[2:], lambda i, g: (i, g, 0, 0, 0)),
        ],
        out_specs=[pl.BlockSpec((n, gw), lambda i, g: (i, g)), pl.BlockSpec((n, gw), lambda i, g: (i, g))],
        out_shape=[out, out],
        scratch_shapes=[
            pltpu.VMEM((2 * hb, 2 * nc * GDN_CHUNK, LANES), BF16),
            pltpu.VMEM((2 * hb, 2 * nc * GDN_CHUNK, LANES), BF16),
            pltpu.VMEM((2 * hb, GDN_DK, GDN_DV), F32),
        ],
        compiler_params=pltpu.CompilerParams(dimension_semantics=("arbitrary", "arbitrary"),
                                             vmem_limit_bytes=VMEM_LIMIT),
    )(p_lat, p_lat, p_lat, p_ctx, p_ctx, gpack)


def _merge_kernel(x_ref, a_ref, of_ref, ob_ref, z_ref, ga_ref, gd_ref, g1_ref, nw_ref, wpa_ref, wpd_ref, wout_ref,
                  o_ref):
    ya = _dot(a_ref[...], wpa_ref[...])
    heads = []
    for hh in range(GDN_HEADS):
        cols = slice(hh * LANES, (hh + 1) * LANES)
        o = of_ref[:, cols].astype(F32) + ob_ref[:, cols].astype(F32)
        o = o * lax.rsqrt(jnp.mean(o * o, axis=-1, keepdims=True) + NORM_EPS) * nw_ref[...]
        heads.append((o * _silu(z_ref[:, cols].astype(F32))).astype(BF16))
    yd = _dot(jnp.concatenate(heads, axis=1), wpd_ref[...])
    y = ga_ref[...].astype(F32) * ya + gd_ref[...].astype(F32) * yd
    o_ref[...] = x_ref[...] + g1_ref[0] * _dot(y.astype(BF16), wout_ref[...])


def _merge(x2, attn, o_f, o_b, p_lat, g1, nw, wpa, wpd, wout, *, n, tm):
    m, d = x2.shape
    per_seq = n // tm
    row = lambda i: (i, 0)
    const = lambda i: (0, 0)
    return pl.pallas_call(
        _merge_kernel,
        grid=(m // tm,),
        in_specs=[
            pl.BlockSpec((tm, d), row),
            pl.BlockSpec((tm, d), row),
            pl.BlockSpec((tm, d), row),
            pl.BlockSpec((tm, d), row),
            pl.BlockSpec((tm, d), lambda i: (i, COL_Z // d)),
            pl.BlockSpec((tm, d), lambda i: (i, COL_GATES // d)),
            pl.BlockSpec((tm, d), lambda i: (i, COL_GATES // d + 1)),
            pl.BlockSpec((1, 1, d), lambda i: (i // per_seq, 0, 0)),
            pl.BlockSpec((1, LANES), const),
            pl.BlockSpec((d, d), const),
            pl.BlockSpec((d, d), const),
            pl.BlockSpec((d, d), const),
        ],
        out_specs=pl.BlockSpec((tm, d), row),
        out_shape=jax.ShapeDtypeStruct((m, d), F32),
        compiler_params=pltpu.CompilerParams(dimension_semantics=("arbitrary",), vmem_limit_bytes=VMEM_LIMIT),
    )(x2, attn, o_f, o_b, p_lat, p_lat, p_lat, g1, nw, wpa, wpd, wout)


def _ffn_kernel(x_ref, sh_ref, sc_ref, g2_ref, wg_ref, wv_ref, cwg_ref, cwv_ref, cbg_ref, cbv_ref, wd_ref, fw_ref,
                o_ref, h_ref, *, seq_len, rb):
    j = pl.program_id(1)
    tm = h_ref.shape[0]
    assert tm % rb == 0

    @pl.when(j == 0)
    def _():
        x = x_ref[...]
        r = lax.rsqrt(jnp.mean(x * x, axis=-1, keepdims=True) + NORM_EPS)
        h_ref[...] = ((x * r) * (1.0 + sc_ref[0]) + sh_ref[0]).astype(BF16)
        o_ref[...] = jnp.zeros_like(o_ref)

    wg = wg_ref[...].astype(BF16)
    wv = wv_ref[...].astype(BF16)
    wd = wd_ref[...].astype(BF16)

    def produce(r):
        _, lo, hi = _block_rows(r, rb, tm, seq_len)
        h = h_ref[lo:hi, :]
        return _dot(h, wg), _dot(h, wv)

    def consume(r, up):
        r0, lo, _ = _block_rows(r, rb, tm, seq_len)
        ug = _conv3_block(up[0], cwg_ref, r0, lo, seq_len, rb) + cbg_ref[...]
        uv = _conv3_block(up[1], cwv_ref, r0, lo, seq_len, rb) + cbv_ref[...]
        o_ref[r0:r0 + rb, :] += _dot((_silu(ug) * uv).astype(BF16), wd)

    _skewed(tm // rb, produce, consume)

    @pl.when(j == pl.num_programs(1) - 1)
    def _():
        y = x_ref[...] + g2_ref[0] * o_ref[...]
        o_ref[...] = y * lax.rsqrt(jnp.mean(y * y, axis=-1, keepdims=True) + NORM_EPS) * fw_ref[...]


def _ffn(x1, sh2, sc2, g2, wup, cw, cb, wdown, fw, *, n, tf, rb):
    m, d = x1.shape
    dff = wdown.shape[0]
    nj = dff // tf
    mod_map = lambda i, j: (i, 0, 0)
    return pl.pallas_call(
        functools.partial(_ffn_kernel, seq_len=n, rb=min(rb, n)),
        grid=(m // n, nj),
        in_specs=[
            pl.BlockSpec((n, d), lambda i, j: (i, 0)),
            pl.BlockSpec((1, 1, d), mod_map),
            pl.BlockSpec((1, 1, d), mod_map),
            pl.BlockSpec((1, 1, d), mod_map),
            pl.BlockSpec((d, tf), lambda i, j: (0, j)),
            pl.BlockSpec((d, tf), lambda i, j: (0, j + nj)),
            pl.BlockSpec((3, tf), lambda i, j: (0, j)),
            pl.BlockSpec((3, tf), lambda i, j: (0, j + nj)),
            pl.BlockSpec((1, tf), lambda i, j: (0, j)),
            pl.BlockSpec((1, tf), lambda i, j: (0, j + nj)),
            pl.BlockSpec((tf, d), lambda i, j: (j, 0)),
            pl.BlockSpec((1, d), lambda i, j: (0, 0)),
        ],
        out_specs=pl.BlockSpec((n, d), lambda i, j: (i, 0)),
        out_shape=jax.ShapeDtypeStruct((m, d), F32),
        scratch_shapes=[pltpu.VMEM((n, d), BF16)],
        compiler_params=pltpu.CompilerParams(dimension_semantics=("arbitrary", "arbitrary"),
                                             vmem_limit_bytes=VMEM_LIMIT),
    )(x1, sh2, sc2, g2, wup, wup, cw, cw, cb, cb, wdown, fw)


def _rope_tables(n):
    rows = n // GRID_W
    row_ids = jnp.broadcast_to(jnp.arange(rows, dtype=F32)[:, None], (rows, GRID_W)).reshape(n)
    col_ids = jnp.broadcast_to(jnp.arange(GRID_W, dtype=F32)[None, :], (rows, GRID_W)).reshape(n)
    inv_freq = ROPE_THETA ** (-jnp.arange(0, ROPE_AXIS_DIM, 2, dtype=F32) / ROPE_AXIS_DIM)
    ar = row_ids[:, None] * inv_freq
    ac = col_ids[:, None] * inv_freq
    cos = jnp.concatenate([jnp.cos(ar), jnp.cos(ac), jnp.cos(ar), jnp.cos(ac)], axis=-1)
    sin = jnp.concatenate([-jnp.sin(ar), -jnp.sin(ac), jnp.sin(ar), jnp.sin(ac)], axis=-1)
    return cos, sin


def _gained_rope_tables(cos, sin, gain, scale):
    return cos * (gain * scale), sin * (jnp.roll(gain, HEAD_DIM // 2) * scale)


def kernel(x, c, ctx, c_ctx, w_mod, b_mod, w_in, q_norm_w, k_norm_w, conv_qkv_w, a_log, dt_bias, gdn_norm_w, w_pa,
           w_pd, w_out, w_up, ffn_conv_w, ffn_conv_b, w_down, final_norm_w):
    assert w_mod.shape[0] == 1, "single-layer block"
    b, n, d = x.shape
    cl = ctx.shape[1]
    hb = 4

    pad = (-(b + 1)) % 8
    cc = jnp.concatenate([c, c_ctx[None, :], jnp.zeros((pad, d), F32)], axis=0)
    mod = _modulation(cc, w_mod[0], b_mod[0][None, :])
    sh1, sc1, g1, sh2, sc2, g2 = [t[:, None, :] for t in jnp.split(mod[:b], 6, axis=-1)]
    csh1, csc1 = mod[b:b + 1, None, :d], mod[b:b + 1, None, d:2 * d]

    w = w_in[0]
    akv, qkv_w = ATTN_KV_HEADS * HEAD_DIM, 3 * GDN_HEADS * GDN_DK
    o_qkv = 2 * akv
    o_db = o_qkv + qkv_w
    o_aq = o_db + 4 * GDN_HEADS
    o_z = o_aq + ATTN_HEADS * HEAD_DIM
    o_gate = o_z + GDN_HEADS * GDN_DV
    w_main = jnp.concatenate([w[:, o_gate:], _pair_major(w[:, o_aq:o_z]), w[:, o_z:o_gate], w[:, o_qkv:o_db],
                              _pair_major(w[:, :akv]), w[:, akv:o_qkv]], axis=1).astype(BF16)
    w_small = jnp.concatenate([w[:, o_db:o_aq], jnp.zeros((d, LANES - 4 * GDN_HEADS), F32)], axis=1).astype(BF16)
    cos, sin = _rope_tables(n)
    qn, kn = _pair_major(q_norm_w), _pair_major(k_norm_w)
    tables = _gained_rope_tables(cos, sin, qn[0], HEAD_DIM ** -0.5) + _gained_rope_tables(cos, sin, kn[0], 1.0)
    x2 = x.reshape(b * n, d)
    common = (w_main, w_small, qn, kn, conv_qkv_w[0], tables)
    p_lat, s_lat = _input_projection(x2, sh1, sc1, *common, tm=n, seq_len=n, tile0=0,
                                     ntiles=MAIN_COLS // PROJ_TILE, rope=True)
    p_ctx, s_ctx = _input_projection(ctx.reshape(b * cl, d), csh1, csc1, *common, tm=b * cl, seq_len=cl,
                                     tile0=CTX_COL0 // PROJ_TILE, ntiles=(MAIN_COLS - CTX_COL0) // PROJ_TILE,
                                     rope=False)

    ncc, ncl = cl // GDN_CHUNK, n // GDN_CHUNK
    nc = ncc + ncl
    ng = GDN_HEADS // hb
    nsm = 4 * GDN_HEADS
    small = jnp.concatenate([s_ctx[:, :nsm].reshape(b, cl, nsm), s_lat[:, :nsm].reshape(b, n, nsm)], axis=1)
    small = small.transpose(0, 2, 1).reshape(b, nsm * nc, GDN_CHUNK)
    alog_rows = jnp.repeat(a_log[0].reshape(-1), nc)[:, None]
    dtb_rows = jnp.repeat(dt_bias[0].reshape(-1), nc)[:, None]
    gq = _gates(small, alog_rows, dtb_rows).reshape(b, 6, 2, ng, hb, nc, GDN_CHUNK)
    gpack = gq.transpose(0, 3, 5, 1, 2, 4, 6).reshape(b, ng, nc, 6 * 2 * hb, GDN_CHUNK)

    attn = _attention(p_lat, p_ctx, b=b, n=n, cl=cl, tq=min(n, 1024))
    o_f, o_b = _gdn(p_lat, p_ctx, gpack, b=b, n=n, cl=cl, hb=hb, cu=2)

    x1 = _merge(x2, attn, o_f, o_b, p_lat, g1, gdn_norm_w, w_pa[0].astype(BF16), w_pd[0].astype(BF16),
                w_out[0].astype(BF16), n=n, tm=min(n, 512))

    out = _ffn(x1, sh2, sc2, g2, w_up[0], ffn_conv_w[0], ffn_conv_b[0][None, :], w_down[0], final_norm_w[None, :],
               n=n, tf=256, rb=4 * ROW_BLOCK)
    return out.reshape(b, n, d)
```

```python
import functools
import math

import jax
import jax.numpy as jnp
from jax import lax
from jax.experimental import pallas as pl
from jax.experimental.pallas import tpu as pltpu

F32 = jnp.float32
BF16 = jnp.bfloat16

LANES = 128
GRID_W = 64
ATTN_HEADS = 8
ATTN_KV_HEADS = 2
ATTN_GROUP = ATTN_HEADS // ATTN_KV_HEADS
HEAD_DIM = 128
ROPE_AXIS_DIM = HEAD_DIM // 2
ROPE_THETA = 10000.0
GDN_HEADS = 8
GDN_DK = 128
GDN_DV = 128
GDN_CHUNK = 128
NORM_EPS = 1e-6
NEG_BIG = -1e30

PROJ_TILE = 512
UNIT_COLS = 256
COL_AK = 0
COL_AV = 256
COL_GQ = 512
COL_GK = 1536
COL_GV = 2560
COL_AQ = 3584
COL_Z = 4608
COL_GATES = 5632
MAIN_COLS = 7680
CTX_COLS = COL_AQ
VMEM_LIMIT = 56 * 1024 * 1024


def _dot(a, b):
    return jnp.dot(a, b, preferred_element_type=F32)


def _dot_nt(a, b):
    return lax.dot_general(a, b, (((1,), (1,)), ((), ())), preferred_element_type=F32)


def _sigmoid(x):
    return 0.5 * jnp.tanh(0.5 * x) + 0.5


def _silu(x):
    h = 0.5 * x
    return h * jnp.tanh(h) + h


def _mod_kernel(c_ref, w_ref, b_ref, o_ref):
    s = _silu(c_ref[...])
    w = w_ref[...]
    s_hi = s.astype(BF16)
    s_lo = (s - s_hi.astype(F32)).astype(BF16)
    w_hi = w.astype(BF16)
    w_lo = (w - w_hi.astype(F32)).astype(BF16)
    o_ref[...] = _dot(s_hi, w_hi) + _dot(s_lo, w_hi) + _dot(s_hi, w_lo) + b_ref[...]


def _modulation(cc, w_mod, b_mod):
    rows, d = cc.shape
    n = w_mod.shape[1]
    tn = 1024
    return pl.pallas_call(
        _mod_kernel,
        grid=(n // tn,),
        in_specs=[pl.BlockSpec((rows, d), lambda j: (0, 0)),
                  pl.BlockSpec((d, tn), lambda j: (0, j)),
                  pl.BlockSpec((1, tn), lambda j: (0, j))],
        out_specs=pl.BlockSpec((rows, tn), lambda j: (0, j)),
        out_shape=jax.ShapeDtypeStruct((rows, n), F32),
    )(cc, w_mod, b_mod)


ROW_BLOCK = 256
ROW_HALO = 16


def _skewed(n, produce, consume):
    prev = produce(0)
    for r in range(1, n):
        cur = produce(r)
        consume(r - 1, prev)
        prev = cur
    consume(n - 1, prev)


def _block_rows(r, rb, tm, seq_len):
    r0 = r * rb
    halo = ROW_HALO if seq_len > rb else 0
    return r0, max(r0 - halo, 0), min(r0 + rb + halo, tm)


def _conv3_block(ext, w_ref, r0, lo, seq_len, rb):
    assert seq_len % rb == 0
    n_ext = ext.shape[0]
    off = r0 - lo
    prev = pltpu.roll(ext, 1, axis=0)[off:off + rb]
    nxt = pltpu.roll(ext, n_ext - 1, axis=0)[off:off + rb]
    row = lax.broadcasted_iota(jnp.int32, (rb, 1), 0)
    if r0 % seq_len == 0:
        prev = jnp.where(row == 0, 0.0, prev)
    if (r0 + rb) % seq_len == 0:
        nxt = jnp.where(row == rb - 1, 0.0, nxt)
    return prev * w_ref[0:1, :] + ext[off:off + rb] * w_ref[1:2, :] + nxt * w_ref[2:3, :]


def _rope(a, cos, sin):
    lane = lax.broadcasted_iota(jnp.int32, a.shape, 1)
    swapped = jnp.where((lane & 32) == 0, pltpu.roll(a, 96, axis=1), pltpu.roll(a, 32, axis=1))
    return a * cos + swapped * sin


def _inproj_kernel(x_ref, sh_ref, sc_ref, wa_ref, wb_ref, ws_ref, qn_ref, kn_ref, cw_ref, qc_ref, qs_ref, kc_ref, ks_ref,
                   o_ref, os_ref, hx_ref, *, tile0, ntiles, seq_len, rope, rb):
    j = pl.program_id(1)
    tm = hx_ref.shape[0]
    assert tm % rb == 0

    @pl.when(j == 0)
    def _():
        x = x_ref[...]
        r = lax.rsqrt(jnp.mean(x * x, axis=-1, keepdims=True) + NORM_EPS)
        h = (x * r) * (1.0 + sc_ref[0]) + sh_ref[0]
        hb = h.astype(BF16)
        hx_ref[...] = hb
        os_ref[...] = _dot(hb, ws_ref[...])

    kind = j + tile0
    nh = UNIT_COLS // LANES

    def head(a, hh):
        return a[:, hh * LANES:(hh + 1) * LANES]

    def store(r0, c0, a):
        o_ref[r0:r0 + rb, c0:c0 + a.shape[1]] = a.astype(BF16)

    def qk_norm_rope(a, w_ref, cos_ref, sin_ref, scale, r0):
        r = lax.rsqrt(jnp.mean(a * a, axis=-1, keepdims=True) + NORM_EPS)
        if rope:
            return r * _rope(a, cos_ref[r0:r0 + rb, :], sin_ref[r0:r0 + rb, :])
        return a * (r * scale) * w_ref[...]

    def ep_gates(acc, r0, lo, c0):
        store(r0, c0, _sigmoid(acc))

    def ep_attn_q(acc, r0, lo, c0):
        for hh in range(nh):
            store(r0, c0 + hh * LANES, qk_norm_rope(head(acc, hh), qn_ref, qc_ref, qs_ref, HEAD_DIM ** -0.5, r0))

    def ep_plain(acc, r0, lo, c0):
        store(r0, c0, acc)

    def conv_silu(ext, r0, lo, c0):
        return _silu(_conv3_block(ext, cw_ref.at[:, c0:c0 + UNIT_COLS], r0, lo, seq_len, rb))

    def ep_gdn_qk(scale):
        def ep(ext, r0, lo, c0):
            y = conv_silu(ext, r0, lo, c0)
            for hh in range(nh):
                a = head(y, hh)
                store(r0, c0 + hh * LANES, a * (lax.rsqrt(jnp.sum(a * a, axis=-1, keepdims=True) + NORM_EPS) * scale))
        return ep

    def ep_gdn_v(ext, r0, lo, c0):
        store(r0, c0, conv_silu(ext, r0, lo, c0))

    def ep_attn_kv(acc, r0, lo, c0):
        if c0 >= ATTN_KV_HEADS * LANES:
            return store(r0, c0, acc)
        for hh in range(nh):
            store(r0, c0 + hh * LANES, qk_norm_rope(head(acc, hh), kn_ref, kc_ref, ks_ref, 1.0, r0))

    bounds = [c // PROJ_TILE for c in (COL_AK, COL_GQ, COL_GK, COL_GV, COL_AQ, COL_Z, COL_GATES, MAIN_COLS)]
    eps = [(ep_attn_kv, False), (ep_gdn_qk(GDN_DK ** -0.5), True), (ep_gdn_qk(1.0), True), (ep_gdn_v, True),
           (ep_attn_q, False), (ep_plain, False), (ep_gates, False)]
    groups = [(lo, hi) + ep for lo, hi, ep in zip(bounds[:-1], bounds[1:], eps)]
    ncol = PROJ_TILE // UNIT_COLS

    for k_lo, k_hi, epilogue, conv in groups:
        if max(k_lo, tile0) >= min(k_hi, tile0 + ntiles):
            continue

        w_ref = wa_ref if k_lo < CTX_COLS // PROJ_TILE else wb_ref

        def run(epilogue=epilogue, conv=conv, w_ref=w_ref):
            def rows(u):
                r = u // ncol
                return _block_rows(r, rb, tm, seq_len) if conv else (r * rb, r * rb, (r + 1) * rb)

            def produce(u):
                _, lo, hi = rows(u)
                c0 = (u % ncol) * UNIT_COLS
                return _dot(hx_ref[lo:hi, :], w_ref[:, c0:c0 + UNIT_COLS])

            def consume(u, acc):
                r0, lo, _ = rows(u)
                epilogue(acc, r0, lo, (u % ncol) * UNIT_COLS)

            _skewed((tm // rb) * ncol, produce, consume)

        pl.when((kind >= k_lo) & (kind < k_hi))(run)


def _input_projection(x2, shift, scale, w_a, w_b, w_small, qn, kn, cw, tables, *, tm, seq_len, tile0, ntiles, rope):
    m, d = x2.shape
    nrow = m // tm
    per_row_mod = shift.shape[0] > 1
    mod_map = (lambda i, j: (i, 0, 0)) if per_row_mod else (lambda i, j: (0, 0, 0))
    conv_tile0 = COL_GQ // PROJ_TILE
    n_conv_tiles = cw.shape[1] // PROJ_TILE
    na, nb = CTX_COLS // PROJ_TILE, w_b.shape[1] // PROJ_TILE
    kern = functools.partial(_inproj_kernel, tile0=tile0, ntiles=ntiles, seq_len=seq_len, rope=rope,
                             rb=min(2 * ROW_BLOCK, seq_len))
    return pl.pallas_call(
        kern,
        grid=(nrow, ntiles),
        in_specs=[
            pl.BlockSpec((tm, d), lambda i, j: (i, 0)),
            pl.BlockSpec((1, 1, d), mod_map),
            pl.BlockSpec((1, 1, d), mod_map),
            pl.BlockSpec((d, PROJ_TILE), lambda i, j: (0, jnp.minimum(j + tile0, na - 1))),
            pl.BlockSpec((d, PROJ_TILE), lambda i, j: (0, jnp.clip(j + tile0 - na, 0, nb - 1))),
            pl.BlockSpec((d, LANES), lambda i, j: (0, 0)),
            pl.BlockSpec((1, LANES), lambda i, j: (0, 0)),
            pl.BlockSpec((1, LANES), lambda i, j: (0, 0)),
            pl.BlockSpec((3, PROJ_TILE), lambda i, j: (0, jnp.clip(j + tile0 - conv_tile0, 0, n_conv_tiles - 1))),
        ] + [pl.BlockSpec(t.shape, lambda i, j: (0, 0), pipeline_mode=pl.Buffered(1)) for t in tables],
        out_specs=[pl.BlockSpec((tm, PROJ_TILE), lambda i, j: (i, j)),
                   pl.BlockSpec((tm, LANES), lambda i, j: (i, 0))],
        out_shape=[jax.ShapeDtypeStruct((m, ntiles * PROJ_TILE), BF16),
                   jax.ShapeDtypeStruct((m, LANES), F32)],
        scratch_shapes=[pltpu.VMEM((tm, d), BF16)],
        compiler_params=pltpu.CompilerParams(dimension_semantics=("arbitrary", "arbitrary"),
                                             vmem_limit_bytes=VMEM_LIMIT),
    )(x2, shift, scale, w_a, w_b, w_small, qn, kn, cw, *tables)


def _gate_kernel(g_ref, alog_ref, dtb_ref, o_ref, *, hb):
    ng, nc = o_ref.shape[1], o_ref.shape[2]
    assert 2 * hb == 8, "one (direction, head) group per sublane tile"
    db = g_ref[0, 0]
    da = g_ref[0, 1]
    beta = _sigmoid(db)
    z = da + dtb_ref[...]
    softplus = jnp.maximum(z, 0.0) + jnp.log(1.0 + jnp.exp(-jnp.abs(z)))
    la = -jnp.exp(alog_ref[...]) * softplus
    lane = lax.broadcasted_iota(jnp.int32, la.shape, 1)
    backward = (lax.broadcasted_iota(jnp.int32, la.shape, 0) // hb) % 2 == 1
    pre = la
    suf = la
    k = 1
    while k < LANES:
        pre = pre + jnp.where(lane >= k, pltpu.roll(pre, k, axis=1), 0.0)
        suf = suf + jnp.where(lane < LANES - k, pltpu.roll(suf, LANES - k, axis=1), 0.0)
        k *= 2
    gam = jnp.where(backward, suf, pre)
    tot = jnp.sum(la, axis=-1, keepdims=True)
    eg = jnp.exp(gam)
    vals = (beta, gam, eg, beta * eg, jnp.exp(tot - gam), jnp.broadcast_to(jnp.exp(tot), gam.shape))
    for qi, v in enumerate(vals):
        o_ref[0, :, :, qi] = v.reshape(ng, nc, 2 * hb, LANES)


def _gates(g, alog_rows, dtb_rows, *, ng, nc, hb):
    b, _, rows, _ = g.shape
    return pl.pallas_call(
        functools.partial(_gate_kernel, hb=hb),
        grid=(b,),
        in_specs=[pl.BlockSpec((1, 2, rows, LANES), lambda i: (i, 0, 0, 0)),
                  pl.BlockSpec((rows, 1), lambda i: (0, 0)),
                  pl.BlockSpec((rows, 1), lambda i: (0, 0))],
        out_specs=pl.BlockSpec((1, ng, nc, 6, 2 * hb, LANES), lambda i: (i, 0, 0, 0, 0, 0)),
        out_shape=jax.ShapeDtypeStruct((b, ng, nc, 6, 2 * hb, LANES), F32),
    )(g, alog_rows, dtb_rows)


def _attn_kernel(q_ref, kc_ref, vc_ref, kx_ref, vx_ref, o_ref):
    kc = kc_ref[...]
    kx = kx_ref[...]
    vc = vc_ref[...]
    vx = vx_ref[...]
    tq = q_ref.shape[0]
    rq = min(tq, 2 * ROW_BLOCK)
    nblk = tq // rq

    def unit(u):
        hh, blk = divmod(u, nblk)
        return slice(blk * rq, (blk + 1) * rq), slice(hh * LANES, (hh + 1) * LANES)

    def scores(u):
        rows, cols = unit(u)
        q = q_ref[rows, cols]
        return _dot_nt(q, kc), _dot_nt(q, kx)

    def softmax_pv(u, s):
        rows, cols = unit(u)
        sc, sx = s
        m = jnp.maximum(jnp.max(sc, axis=-1, keepdims=True), jnp.max(sx, axis=-1, keepdims=True))
        pc = jnp.exp(sc - m)
        px = jnp.exp(sx - m)
        denom = jnp.sum(pc, axis=-1, keepdims=True) + jnp.sum(px, axis=-1, keepdims=True)
        o = _dot(pc.astype(BF16), vc) + _dot(px.astype(BF16), vx)
        o_ref[rows, cols] = (o / denom).astype(BF16)

    _skewed(ATTN_GROUP * nblk, scores, softmax_pv)


def _attention(p_lat, p_ctx, *, b, n, cl, tq):
    gw = ATTN_GROUP * LANES
    nq = n // tq
    q0 = COL_AQ // gw
    kx0, vx0 = COL_AK // LANES, COL_AV // LANES
    return pl.pallas_call(
        _attn_kernel,
        grid=(b, ATTN_KV_HEADS, nq),
        in_specs=[
            pl.BlockSpec((tq, gw), lambda i, h, t: (i * nq + t, q0 + h)),
            pl.BlockSpec((cl, LANES), lambda i, h, t: (i, kx0 + h)),
            pl.BlockSpec((cl, LANES), lambda i, h, t: (i, vx0 + h)),
            pl.BlockSpec((n, LANES), lambda i, h, t: (i, kx0 + h)),
            pl.BlockSpec((n, LANES), lambda i, h, t: (i, vx0 + h)),
        ],
        out_specs=pl.BlockSpec((tq, gw), lambda i, h, t: (i * nq + t, h)),
        out_shape=jax.ShapeDtypeStruct((b * n, ATTN_HEADS * LANES), BF16),
        compiler_params=pltpu.CompilerParams(dimension_semantics=("arbitrary", "arbitrary", "arbitrary"),
                                             vmem_limit_bytes=VMEM_LIMIT),
    )(p_lat, p_ctx, p_ctx, p_lat, p_lat)


G_BETA, G_GAM, G_EG, G_BEG, G_EGD, G_GL = range(6)


def _gdn_kernel(ql_ref, kl_ref, vl_ref, kc_ref, vc_ref, g_ref, of_ref, ob_ref, mq_ref, bo_ref, s_ref,
                *, hb, ncc, ncl, cu):
    c_sz = GDN_CHUNK
    ri = lax.broadcasted_iota(jnp.int32, (c_sz, c_sz), 0)
    ci = lax.broadcasted_iota(jnp.int32, (c_sz, c_sz), 1)
    eye = (ri == ci).astype(F32)
    levels = int(math.log2(c_sz))
    cu_ctx = math.gcd(ncc, cu)
    cu_lat = math.gcd(ncl, cu)

    def tri_masks(d):
        lo, hi = (ci, ri) if d == 0 else (ri, ci)
        incl = hi >= lo
        strict = hi > lo
        pair = []
        for lv in range(levels):
            bh = hi >> lv
            bl = lo >> lv
            pair.append(((bh & 1) == 1) & (bl == bh - 1))
        return incl, strict, pair

    masks = [tri_masks(0), tri_masks(1)]

    def grow(gt, qi, d, hh):
        return gt[qi, d * hb + hh:d * hb + hh + 1, :]

    def prep(chunks):
        inst = []
        for c, k, v, q in chunks:
            rq = pl.multiple_of(c * (2 * c_sz), 2 * c_sz)
            gt = g_ref[0, 0, c]
            gam_rows = gt[G_GAM]
            gam_cols = jnp.concatenate([gam_rows, jnp.zeros((c_sz - 2 * hb, c_sz), F32)], axis=0).T
            for hh in range(hb):
                kh = k[hh]
                kk = _dot_nt(kh, kh)
                qk = None if q is None else _dot_nt(q[hh], kh)
                kt = kh.astype(F32).T
                for d in range(2):
                    inst.append(dict(rq=rq, gt=gt, hh=hh, d=d, hd=hh * 2 + d, kh=kh, vh=v[hh], kk=kk, qk=qk, kt=kt,
                                     qh=None if q is None else q[hh],
                                     gam_c=gam_cols[:, d * hb + hh:d * hb + hh + 1]))
        for it in inst:
            d, hh, gt = it["d"], it["hh"], it["gt"]
            incl, strict, pair = masks[d]
            beta_r = grow(gt, G_BETA, d, hh)
            e = jnp.exp(jnp.where(incl, it["gam_c"] - grow(gt, G_GAM, d, hh), NEG_BIG))
            l2 = jnp.where(strict, e * it["kk"], 0.0) * beta_r
            it["t"] = eye - jnp.where(pair[0], l2, 0.0)
            it["ms"] = [jnp.where(pair[lv], l2, 0.0).astype(BF16) for lv in range(1, levels)]
            it["kb"] = (it["kt"] * (grow(gt, G_EGD, d, hh) * beta_r)).astype(BF16)
            if it["qk"] is not None:
                pb = (e * it["qk"] * beta_r).astype(BF16)
                it["pb"] = pb
                it["qlhs"] = jnp.concatenate([(eye * grow(gt, G_EG, d, hh)).astype(BF16), -pb], axis=1)
        for lv in range(levels - 1):
            xs = [_dot(it["ms"][lv], it["t"].astype(BF16)).astype(BF16) for it in inst]
            for it, xm in zip(inst, xs):
                it["t"] = it["t"] - _dot(it["t"].astype(BF16), xm)
        for it in inst:
            it["yu"] = _dot(it["t"].astype(BF16), it["vh"]).astype(BF16)
        for it in inst:
            it["yw"] = _dot((it["t"] * grow(it["gt"], G_EG, it["d"], it["hh"])).astype(BF16), it["kh"]).astype(BF16)
        for it in inst:
            bm = _dot(it["kb"], jnp.concatenate([it["yu"], it["yw"]], axis=1))
            mq_ref[it["hd"], pl.ds(it["rq"], c_sz), :] = (-bm[:, GDN_DV:]).astype(BF16)
            bo_ref[it["hd"], pl.ds(it["rq"], c_sz), :] = bm[:, :GDN_DV].astype(BF16)
        for it in inst:
            if it["qk"] is not None:
                oo = _dot(it["pb"], it["yu"])
                qq = _dot(it["qlhs"], jnp.concatenate([it["qh"], it["yw"]], axis=0))
                mq_ref[it["hd"], pl.ds(it["rq"] + c_sz, c_sz), :] = qq.astype(BF16)
                bo_ref[it["hd"], pl.ds(it["rq"] + c_sz, c_sz), :] = oo.astype(BF16)

    def split_heads(ref, r0):
        return [ref[pl.ds(r0, c_sz), hh * LANES:(hh + 1) * LANES] for hh in range(hb)]

    def prep_ctx(i, carry):
        chunks = []
        for j in range(cu_ctx):
            c = i * cu_ctx + j
            r0 = pl.multiple_of(c * c_sz, c_sz)
            chunks.append((c, split_heads(kc_ref, r0), split_heads(vc_ref, r0), None))
        prep(chunks)
        return carry

    def prep_lat(i, carry):
        chunks = []
        for j in range(cu_lat):
            c = i * cu_lat + j
            r0 = pl.multiple_of(c * c_sz, c_sz)
            chunks.append((c + ncc, split_heads(kl_ref, r0), split_heads(vl_ref, r0), split_heads(ql_ref, r0)))
        prep(chunks)
        return carry

    lax.fori_loop(0, ncc // cu_ctx, prep_ctx, 0)
    lax.fori_loop(0, ncl // cu_lat, prep_lat, 0)

    s_ref[...] = jnp.zeros_like(s_ref)

    def scan_steps(steps):
        st = []
        for hh, d, c, out_row in steps:
            hd = hh * 2 + d
            rq = pl.multiple_of(c * (2 * c_sz), 2 * c_sz)
            rows = c_sz if out_row is None else 2 * c_sz
            s = s_ref[hd]
            st.append(dict(hd=hd, hh=hh, d=d, s=s, rq=rq, rows=rows, out_row=out_row,
                           gl=g_ref[0, 0, c, G_GL, pl.ds(d * hb + hh, 1), :],
                           x=_dot(mq_ref[hd, pl.ds(rq, rows), :], s.astype(BF16))))
        for it in st:
            bo = bo_ref[it["hd"], pl.ds(it["rq"], it["rows"]), :].astype(F32)
            s_ref[it["hd"]] = it["gl"] * it["s"] + it["x"][:c_sz] + bo[:c_sz]
            if it["out_row"] is not None:
                out_ref = of_ref if it["d"] == 0 else ob_ref
                out_ref[pl.ds(it["out_row"], c_sz), it["hh"] * LANES:(it["hh"] + 1) * LANES] = (
                    it["x"][c_sz:] + bo[c_sz:]).astype(BF16)

    def scan_ctx(i, carry):
        steps = []
        for hh in range(hb):
            steps += [(hh, 0, i, None), (hh, 1, ncc - 1 - i, None)]
        scan_steps(steps)
        return carry

    def scan_lat(i, carry):
        steps = []
        for hh in range(hb):
            steps += [(hh, 0, ncc + i, pl.multiple_of(i * c_sz, c_sz)),
                      (hh, 1, ncc + ncl - 1 - i, pl.multiple_of((ncl - 1 - i) * c_sz, c_sz))]
        scan_steps(steps)
        return carry

    lax.fori_loop(0, ncc, scan_ctx, 0)
    lax.fori_loop(0, ncl, scan_lat, 0)


def _gdn(p_lat, p_ctx, gpack, *, b, n, cl, hb, cu):
    gw = hb * LANES
    ng = GDN_HEADS // hb
    ncc, ncl = cl // GDN_CHUNK, n // GDN_CHUNK
    nc = ncc + ncl
    q0, k0, v0 = COL_GQ // gw, COL_GK // gw, COL_GV // gw
    kern = functools.partial(_gdn_kernel, hb=hb, ncc=ncc, ncl=ncl, cu=cu)
    out = jax.ShapeDtypeStruct((b * n, GDN_HEADS * LANES), BF16)
    return pl.pallas_call(
        kern,
        grid=(b, ng),
        in_specs=[
            pl.BlockSpec((n, gw), lambda i, g: (i, q0 + g)),
            pl.BlockSpec((n, gw), lambda i, g: (i, k0 + g)),
            pl.BlockSpec((n, gw), lambda i, g: (i, v0 + g)),
            pl.BlockSpec((cl, gw), lambda i, g: (i, k0 + g)),
            pl.BlockSpec((cl, gw), lambda i, g: (i, v0 + g)),
            pl.BlockSpec((1, 1) + gpack.shape[2:], lambda i, g: (i, g, 0, 0, 0, 0)),
        ],
        out_specs=[pl.BlockSpec((n, gw), lambda i, g: (i, g)), pl.BlockSpec((n, gw), lambda i, g: (i, g))],
        out_shape=[out, out],
        scratch_shapes=[
            pltpu.VMEM((2 * hb, 2 * nc * GDN_CHUNK, LANES), BF16),
            pltpu.VMEM((2 * hb, 2 * nc * GDN_CHUNK, LANES), BF16),
            pltpu.VMEM((2 * hb, GDN_DK, GDN_DV), F32),
        ],
        compiler_params=pltpu.CompilerParams(dimension_semantics=("arbitrary", "arbitrary"),
                                             vmem_limit_bytes=VMEM_LIMIT),
    )(p_lat, p_lat, p_lat, p_ctx, p_ctx, gpack)


def _merge_kernel(x_ref, a_ref, of_ref, ob_ref, z0_ref, z1_ref, ga0_ref, ga1_ref, gd0_ref, gd1_ref, g1_ref, nw_ref,
                  wpa_ref, wpd_ref, wout_ref, o_ref):
    ya = _dot(a_ref[...], wpa_ref[...])
    heads = []
    per_half = GDN_HEADS // 2
    for hh in range(GDN_HEADS):
        cols = slice(hh * LANES, (hh + 1) * LANES)
        zcols = slice((hh % per_half) * LANES, (hh % per_half + 1) * LANES)
        z = (z0_ref if hh < per_half else z1_ref)[:, zcols].astype(F32)
        o = of_ref[:, cols].astype(F32) + ob_ref[:, cols].astype(F32)
        o = o * lax.rsqrt(jnp.mean(o * o, axis=-1, keepdims=True) + NORM_EPS) * nw_ref[...]
        heads.append((o * _silu(z)).astype(BF16))
    yd = _dot(jnp.concatenate(heads, axis=1), wpd_ref[...])
    ga = jnp.concatenate([ga0_ref[...], ga1_ref[...]], axis=1).astype(F32)
    gd = jnp.concatenate([gd0_ref[...], gd1_ref[...]], axis=1).astype(F32)
    y = ga * ya + gd * yd
    o_ref[...] = x_ref[...] + g1_ref[0] * _dot(y.astype(BF16), wout_ref[...])


def _merge(x2, attn, o_f, o_b, p_lat, g1, nw, wpa, wpd, wout, *, n, tm):
    m, d = x2.shape
    per_seq = n // tm
    row = lambda i: (i, 0)
    const = lambda i: (0, 0)
    return pl.pallas_call(
        _merge_kernel,
        grid=(m // tm,),
        in_specs=[
            pl.BlockSpec((tm, d), row),
            pl.BlockSpec((tm, d), row),
            pl.BlockSpec((tm, d), row),
            pl.BlockSpec((tm, d), row),
        ] + [pl.BlockSpec((tm, PROJ_TILE), functools.partial(lambda i, t: (i, t), t=c0 // PROJ_TILE + k))
             for c0 in (COL_Z, COL_GATES, COL_GATES + d) for k in range(2)] + [
            pl.BlockSpec((1, 1, d), lambda i: (i // per_seq, 0, 0)),
            pl.BlockSpec((1, LANES), const),
            pl.BlockSpec((d, d), const),
            pl.BlockSpec((d, d), const),
            pl.BlockSpec((d, d), const),
        ],
        out_specs=pl.BlockSpec((tm, d), row),
        out_shape=jax.ShapeDtypeStruct((m, d), F32),
        compiler_params=pltpu.CompilerParams(dimension_semantics=("arbitrary",), vmem_limit_bytes=VMEM_LIMIT),
    )(x2, attn, o_f, o_b, *([p_lat] * 6), g1, nw, wpa, wpd, wout)


def _ffn_kernel(x_ref, sh_ref, sc_ref, g2_ref, wg_ref, wv_ref, cwg_ref, cwv_ref, cbg_ref, cbv_ref, wd_ref, fw_ref,
                o_ref, h_ref, *, seq_len, rb):
    j = pl.program_id(1)
    tm = h_ref.shape[0]
    assert tm % rb == 0

    @pl.when(j == 0)
    def _():
        x = x_ref[...]
        r = lax.rsqrt(jnp.mean(x * x, axis=-1, keepdims=True) + NORM_EPS)
        h_ref[...] = ((x * r) * (1.0 + sc_ref[0]) + sh_ref[0]).astype(BF16)
        o_ref[...] = jnp.zeros_like(o_ref)

    wg = wg_ref[...].astype(BF16)
    wv = wv_ref[...].astype(BF16)
    wd = wd_ref[...].astype(BF16)

    def produce(r):
        _, lo, hi = _block_rows(r, rb, tm, seq_len)
        h = h_ref[lo:hi, :]
        return _dot(h, wg), _dot(h, wv)

    def consume(r, up):
        r0, lo, _ = _block_rows(r, rb, tm, seq_len)
        ug = _conv3_block(up[0], cwg_ref, r0, lo, seq_len, rb) + cbg_ref[...]
        uv = _conv3_block(up[1], cwv_ref, r0, lo, seq_len, rb) + cbv_ref[...]
        o_ref[r0:r0 + rb, :] += _dot((_silu(ug) * uv).astype(BF16), wd)

    _skewed(tm // rb, produce, consume)

    @pl.when(j == pl.num_programs(1) - 1)
    def _():
        y = x_ref[...] + g2_ref[0] * o_ref[...]
        o_ref[...] = y * lax.rsqrt(jnp.mean(y * y, axis=-1, keepdims=True) + NORM_EPS) * fw_ref[...]


def _ffn(x1, sh2, sc2, g2, wup, cw, cb, wdown, fw, *, n, tf, rb):
    m, d = x1.shape
    dff = wdown.shape[0]
    nj = dff // tf
    mod_map = lambda i, j: (i, 0, 0)
    return pl.pallas_call(
        functools.partial(_ffn_kernel, seq_len=n, rb=min(rb, n)),
        grid=(m // n, nj),
        in_specs=[
            pl.BlockSpec((n, d), lambda i, j: (i, 0)),
            pl.BlockSpec((1, 1, d), mod_map),
            pl.BlockSpec((1, 1, d), mod_map),
            pl.BlockSpec((1, 1, d), mod_map),
            pl.BlockSpec((d, tf), lambda i, j: (0, j)),
            pl.BlockSpec((d, tf), lambda i, j: (0, j + nj)),
            pl.BlockSpec((3, tf), lambda i, j: (0, j)),
            pl.BlockSpec((3, tf), lambda i, j: (0, j + nj)),
            pl.BlockSpec((1, tf), lambda i, j: (0, j)),
            pl.BlockSpec((1, tf), lambda i, j: (0, j + nj)),
            pl.BlockSpec((tf, d), lambda i, j: (j, 0)),
            pl.BlockSpec((1, d), lambda i, j: (0, 0)),
        ],
        out_specs=pl.BlockSpec((n, d), lambda i, j: (i, 0)),
        out_shape=jax.ShapeDtypeStruct((m, d), F32),
        scratch_shapes=[pltpu.VMEM((n, d), BF16)],
        compiler_params=pltpu.CompilerParams(dimension_semantics=("arbitrary", "arbitrary"),
                                             vmem_limit_bytes=VMEM_LIMIT),
    )(x1, sh2, sc2, g2, wup, wup, cw, cw, cb, cb, wdown, fw)


def _rope_tables(n):
    rows = n // GRID_W
    row_ids = jnp.broadcast_to(jnp.arange(rows, dtype=F32)[:, None], (rows, GRID_W)).reshape(n)
    col_ids = jnp.broadcast_to(jnp.arange(GRID_W, dtype=F32)[None, :], (rows, GRID_W)).reshape(n)
    inv_freq = ROPE_THETA ** (-jnp.arange(0, ROPE_AXIS_DIM, 2, dtype=F32) / ROPE_AXIS_DIM)
    ar = row_ids[:, None] * inv_freq
    ac = col_ids[:, None] * inv_freq
    cos = jnp.concatenate([jnp.cos(ar), jnp.cos(ar), jnp.cos(ac), jnp.cos(ac)], axis=-1)
    sin = jnp.concatenate([-jnp.sin(ar), jnp.sin(ar), -jnp.sin(ac), jnp.sin(ac)], axis=-1)
    return cos, sin


def _gained_rope_tables(cos, sin, gain, scale):
    swapped_gain = gain.reshape(-1, 2, ROPE_AXIS_DIM // 2)[:, ::-1].reshape(gain.shape)
    return cos * (gain * scale), sin * (swapped_gain * scale)


def kernel(x, c, ctx, c_ctx, w_mod, b_mod, w_in, q_norm_w, k_norm_w, conv_qkv_w, a_log, dt_bias, gdn_norm_w, w_pa,
           w_pd, w_out, w_up, ffn_conv_w, ffn_conv_b, w_down, final_norm_w):
    assert w_mod.shape[0] == 1, "single-layer block"
    b, n, d = x.shape
    cl = ctx.shape[1]
    hb = 4

    pad = (-(b + 1)) % 8
    cc = jnp.concatenate([c, c_ctx[None, :], jnp.zeros((pad, d), F32)], axis=0)
    mod = _modulation(cc, w_mod[0], b_mod[0][None, :])
    sh1, sc1, g1, sh2, sc2, g2 = [t[:, None, :] for t in jnp.split(mod[:b], 6, axis=-1)]
    csh1, csc1 = mod[b:b + 1, None, :d], mod[b:b + 1, None, d:2 * d]

    w = w_in[0]
    akv, qkv_w = ATTN_KV_HEADS * HEAD_DIM, 3 * GDN_HEADS * GDN_DK
    o_qkv = 2 * akv
    o_db = o_qkv + qkv_w
    o_aq = o_db + 4 * GDN_HEADS
    w16 = w.astype(BF16)
    w_behind = w16[:, o_aq:]
    w_small = jnp.pad(w16[:, o_db:o_aq], ((0, 0), (0, LANES - 4 * GDN_HEADS)))
    cos, sin = _rope_tables(n)
    qn, kn = q_norm_w, k_norm_w
    tables = _gained_rope_tables(cos, sin, qn[0], HEAD_DIM ** -0.5) + _gained_rope_tables(cos, sin, kn[0], 1.0)
    x2 = x.reshape(b * n, d)
    common = (w16, w_behind, w_small, qn, kn, conv_qkv_w[0], tables)
    p_lat, s_lat = _input_projection(x2, sh1, sc1, *common, tm=n, seq_len=n, tile0=0,
                                     ntiles=MAIN_COLS // PROJ_TILE, rope=True)
    p_ctx, s_ctx = _input_projection(ctx.reshape(b * cl, d), csh1, csc1, *common, tm=b * cl, seq_len=cl,
                                     tile0=0, ntiles=CTX_COLS // PROJ_TILE, rope=False)

    ncc, ncl = cl // GDN_CHUNK, n // GDN_CHUNK
    nc = ncc + ncl
    ng = GDN_HEADS // hb
    nsm = 4 * GDN_HEADS
    small = jnp.concatenate([s_ctx[:, :nsm].reshape(b, cl, nsm), s_lat[:, :nsm].reshape(b, n, nsm)], axis=1)
    small = small.reshape(b, nc, GDN_CHUNK, 2, 2, ng, hb).transpose(0, 3, 5, 1, 4, 6, 2)
    small = small.reshape(b, 2, ng * nc * 2 * hb, GDN_CHUNK)

    def per_row(p):
        p = jnp.broadcast_to(p.reshape(2, ng, 1, hb).transpose(1, 2, 0, 3), (ng, nc, 2, hb))
        return p.reshape(-1, 1)

    gpack = _gates(small, per_row(a_log[0]), per_row(dt_bias[0]), ng=ng, nc=nc, hb=hb)

    attn = _attention(p_lat, p_ctx, b=b, n=n, cl=cl, tq=min(n, 1024))
    o_f, o_b = _gdn(p_lat, p_ctx, gpack, b=b, n=n, cl=cl, hb=hb, cu=2)

    x1 = _merge(x2, attn, o_f, o_b, p_lat, g1, gdn_norm_w, w_pa[0].astype(BF16), w_pd[0].astype(BF16),
                w_out[0].astype(BF16), n=n, tm=min(n, 512))

    out = _ffn(x1, sh2, sc2, g2, w_up[0], ffn_conv_w[0], ffn_conv_b[0][None, :], w_down[0], final_norm_w[None, :],
               n=n, tf=256, rb=4 * ROW_BLOCK)
    return out.reshape(b, n, d)
```

```python
import functools
import math

import jax
import jax.numpy as jnp
from jax import lax
from jax.experimental import pallas as pl
from jax.experimental.pallas import tpu as pltpu

F32 = jnp.float32
BF16 = jnp.bfloat16

LANES = 128
GRID_W = 64
ATTN_HEADS = 8
ATTN_KV_HEADS = 2
ATTN_GROUP = ATTN_HEADS // ATTN_KV_HEADS
HEAD_DIM = 128
ROPE_AXIS_DIM = HEAD_DIM // 2
ROPE_THETA = 10000.0
GDN_HEADS = 8
GDN_DK = 128
GDN_DV = 128
GDN_CHUNK = 128
NORM_EPS = 1e-6
SCORE_SCALE = HEAD_DIM ** -0.5 * math.log2(math.e)
NEG_BIG = -1e30

PROJ_TILE = 512
UNIT_COLS = 256
COL_AK = 0
COL_AV = 256
COL_GQ = 512
COL_GK = 1536
COL_GV = 2560
COL_AQ = 3584
COL_Z = 4608
COL_GATES = 5632
MAIN_COLS = 7680
CTX_COLS = COL_AQ
VMEM_LIMIT = 56 * 1024 * 1024


def _dot(a, b):
    return jnp.dot(a, b, preferred_element_type=F32)


def _dot_nt(a, b):
    return lax.dot_general(a, b, (((1,), (1,)), ((), ())), preferred_element_type=F32)


def _sigmoid(x):
    return 0.5 * jnp.tanh(0.5 * x) + 0.5


def _silu(x):
    h = 0.5 * x
    return h * jnp.tanh(h) + h


def _mod_kernel(c_ref, w_ref, b_ref, o_ref):
    s = _silu(c_ref[...])
    w = w_ref[...]
    s_hi = s.astype(BF16)
    s_lo = (s - s_hi.astype(F32)).astype(BF16)
    w_hi = w.astype(BF16)
    w_lo = (w - w_hi.astype(F32)).astype(BF16)
    o_ref[...] = _dot(s_hi, w_hi) + _dot(s_lo, w_hi) + _dot(s_hi, w_lo) + b_ref[...]


def _modulation(cc, w_mod, b_mod):
    rows, d = cc.shape
    n = w_mod.shape[1]
    tn = 1024
    return pl.pallas_call(
        _mod_kernel,
        grid=(n // tn,),
        in_specs=[pl.BlockSpec((rows, d), lambda j: (0, 0)),
                  pl.BlockSpec((d, tn), lambda j: (0, j)),
                  pl.BlockSpec((1, tn), lambda j: (0, j))],
        out_specs=pl.BlockSpec((rows, tn), lambda j: (0, j)),
        out_shape=jax.ShapeDtypeStruct((rows, n), F32),
    )(cc, w_mod, b_mod)


ROW_BLOCK = 256
ROW_HALO = 16


def _skewed(n, produce, consume):
    prev = produce(0)
    for r in range(1, n):
        cur = produce(r)
        consume(r - 1, prev)
        prev = cur
    consume(n - 1, prev)


def _block_rows(r, rb, tm, seq_len):
    r0 = r * rb
    halo = ROW_HALO if seq_len > rb else 0
    return r0, max(r0 - halo, 0), min(r0 + rb + halo, tm)


def _conv3_block(ext, w_ref, r0, lo, seq_len, rb):
    assert seq_len % rb == 0
    n_ext = ext.shape[0]
    off = r0 - lo
    prev = pltpu.roll(ext, 1, axis=0)[off:off + rb]
    nxt = pltpu.roll(ext, n_ext - 1, axis=0)[off:off + rb]
    row = lax.broadcasted_iota(jnp.int32, (rb, 1), 0)
    if r0 % seq_len == 0:
        prev = jnp.where(row == 0, 0.0, prev)
    if (r0 + rb) % seq_len == 0:
        nxt = jnp.where(row == rb - 1, 0.0, nxt)
    return prev * w_ref[0:1, :] + ext[off:off + rb] * w_ref[1:2, :] + nxt * w_ref[2:3, :]


def _rope(a, cos, sin):
    lane = lax.broadcasted_iota(jnp.int32, a.shape, 1)
    swapped = jnp.where((lane & 32) == 0, pltpu.roll(a, 96, axis=1), pltpu.roll(a, 32, axis=1))
    return a * cos + swapped * sin


def _inproj_kernel(x_ref, sh_ref, sc_ref, wa_ref, wb_ref, ws_ref, qn_ref, kn_ref, cw_ref, qc_ref, qs_ref, kc_ref, ks_ref,
                   o_ref, os_ref, hx_ref, *, tile0, ntiles, seq_len, rope, rb):
    j = pl.program_id(1)
    tm = hx_ref.shape[0]
    assert tm % rb == 0

    @pl.when(j == 0)
    def _():
        x = x_ref[...]
        r = lax.rsqrt(jnp.mean(x * x, axis=-1, keepdims=True) + NORM_EPS)
        h = (x * r) * (1.0 + sc_ref[0]) + sh_ref[0]
        hb = h.astype(BF16)
        hx_ref[...] = hb
        os_ref[...] = _dot(hb, ws_ref[...])

    kind = j + tile0
    nh = UNIT_COLS // LANES

    def head(a, hh):
        return a[:, hh * LANES:(hh + 1) * LANES]

    def store(r0, c0, a):
        o_ref[r0:r0 + rb, c0:c0 + a.shape[1]] = a.astype(BF16)

    def qk_norm_rope(a, w_ref, cos_ref, sin_ref, scale, r0):
        r = lax.rsqrt(jnp.mean(a * a, axis=-1, keepdims=True) + NORM_EPS)
        if rope:
            return r * _rope(a, cos_ref[r0:r0 + rb, :], sin_ref[r0:r0 + rb, :])
        return a * (r * scale) * w_ref[...]

    def ep_gates(acc, r0, lo, c0):
        store(r0, c0, _sigmoid(acc))

    def ep_attn_q(acc, r0, lo, c0):
        for hh in range(nh):
            store(r0, c0 + hh * LANES, qk_norm_rope(head(acc, hh), qn_ref, qc_ref, qs_ref, SCORE_SCALE, r0))

    def ep_plain(acc, r0, lo, c0):
        store(r0, c0, acc)

    def conv_silu(ext, r0, lo, c0):
        return _silu(_conv3_block(ext, cw_ref.at[:, c0:c0 + UNIT_COLS], r0, lo, seq_len, rb))

    def ep_gdn_qk(scale):
        def ep(ext, r0, lo, c0):
            y = conv_silu(ext, r0, lo, c0)
            for hh in range(nh):
                a = head(y, hh)
                store(r0, c0 + hh * LANES, a * (lax.rsqrt(jnp.sum(a * a, axis=-1, keepdims=True) + NORM_EPS) * scale))
        return ep

    def ep_gdn_v(ext, r0, lo, c0):
        store(r0, c0, conv_silu(ext, r0, lo, c0))

    def ep_attn_kv(acc, r0, lo, c0):
        if c0 >= ATTN_KV_HEADS * LANES:
            return store(r0, c0, acc)
        for hh in range(nh):
            store(r0, c0 + hh * LANES, qk_norm_rope(head(acc, hh), kn_ref, kc_ref, ks_ref, 1.0, r0))

    bounds = [c // PROJ_TILE for c in (COL_AK, COL_GQ, COL_GK, COL_GV, COL_AQ, COL_Z, COL_GATES, MAIN_COLS)]
    eps = [(ep_attn_kv, False), (ep_gdn_qk(GDN_DK ** -0.5), True), (ep_gdn_qk(1.0), True), (ep_gdn_v, True),
           (ep_attn_q, False), (ep_plain, False), (ep_gates, False)]
    groups = [(lo, hi) + ep for lo, hi, ep in zip(bounds[:-1], bounds[1:], eps)]
    ncol = PROJ_TILE // UNIT_COLS

    for k_lo, k_hi, epilogue, conv in groups:
        if max(k_lo, tile0) >= min(k_hi, tile0 + ntiles):
            continue

        w_ref = wa_ref if k_lo < CTX_COLS // PROJ_TILE else wb_ref

        def run(epilogue=epilogue, conv=conv, w_ref=w_ref):
            def rows(u):
                r = u // ncol
                return _block_rows(r, rb, tm, seq_len) if conv else (r * rb, r * rb, (r + 1) * rb)

            def produce(u):
                _, lo, hi = rows(u)
                c0 = (u % ncol) * UNIT_COLS
                return _dot(hx_ref[lo:hi, :], w_ref[:, c0:c0 + UNIT_COLS])

            def consume(u, acc):
                r0, lo, _ = rows(u)
                epilogue(acc, r0, lo, (u % ncol) * UNIT_COLS)

            _skewed((tm // rb) * ncol, produce, consume)

        pl.when((kind >= k_lo) & (kind < k_hi))(run)


def _input_projection(x2, shift, scale, w_a, w_b, w_small, qn, kn, cw, tables, *, tm, seq_len, tile0, ntiles, rope):
    m, d = x2.shape
    nrow = m // tm
    per_row_mod = shift.shape[0] > 1
    mod_map = (lambda i, j: (i, 0, 0)) if per_row_mod else (lambda i, j: (0, 0, 0))
    conv_tile0 = COL_GQ // PROJ_TILE
    n_conv_tiles = cw.shape[1] // PROJ_TILE
    na, nb = CTX_COLS // PROJ_TILE, w_b.shape[1] // PROJ_TILE
    kern = functools.partial(_inproj_kernel, tile0=tile0, ntiles=ntiles, seq_len=seq_len, rope=rope,
                             rb=min(2 * ROW_BLOCK, seq_len))
    return pl.pallas_call(
        kern,
        grid=(nrow, ntiles),
        in_specs=[
            pl.BlockSpec((tm, d), lambda i, j: (i, 0)),
            pl.BlockSpec((1, 1, d), mod_map),
            pl.BlockSpec((1, 1, d), mod_map),
            pl.BlockSpec((d, PROJ_TILE), lambda i, j: (0, jnp.minimum(j + tile0, na - 1))),
            pl.BlockSpec((d, PROJ_TILE), lambda i, j: (0, jnp.clip(j + tile0 - na, 0, nb - 1))),
            pl.BlockSpec((d, LANES), lambda i, j: (0, 0)),
            pl.BlockSpec((1, LANES), lambda i, j: (0, 0)),
            pl.BlockSpec((1, LANES), lambda i, j: (0, 0)),
            pl.BlockSpec((3, PROJ_TILE), lambda i, j: (0, jnp.clip(j + tile0 - conv_tile0, 0, n_conv_tiles - 1))),
        ] + [pl.BlockSpec(t.shape, lambda i, j: (0, 0), pipeline_mode=pl.Buffered(1)) for t in tables],
        out_specs=[pl.BlockSpec((tm, PROJ_TILE), lambda i, j: (i, j)),
                   pl.BlockSpec((tm, LANES), lambda i, j: (i, 0))],
        out_shape=[jax.ShapeDtypeStruct((m, ntiles * PROJ_TILE), BF16),
                   jax.ShapeDtypeStruct((m, LANES), F32)],
        scratch_shapes=[pltpu.VMEM((tm, d), BF16)],
        compiler_params=pltpu.CompilerParams(dimension_semantics=("arbitrary", "arbitrary"),
                                             vmem_limit_bytes=VMEM_LIMIT),
    )(x2, shift, scale, w_a, w_b, w_small, qn, kn, cw, *tables)


def _gate_kernel(g_ref, alog_ref, dtb_ref, o_ref, *, hb):
    ng, nc = o_ref.shape[1], o_ref.shape[2]
    assert 2 * hb == 8, "one (direction, head) group per sublane tile"
    db = g_ref[0, 0]
    da = g_ref[0, 1]
    beta = _sigmoid(db)
    z = da + dtb_ref[...]
    softplus = jnp.maximum(z, 0.0) + jnp.log(1.0 + jnp.exp(-jnp.abs(z)))
    la = -jnp.exp(alog_ref[...]) * softplus
    lane = lax.broadcasted_iota(jnp.int32, la.shape, 1)
    backward = (lax.broadcasted_iota(jnp.int32, la.shape, 0) // hb) % 2 == 1
    pre = la
    suf = la
    k = 1
    while k < LANES:
        pre = pre + jnp.where(lane >= k, pltpu.roll(pre, k, axis=1), 0.0)
        suf = suf + jnp.where(lane < LANES - k, pltpu.roll(suf, LANES - k, axis=1), 0.0)
        k *= 2
    gam = jnp.where(backward, suf, pre)
    tot = jnp.sum(la, axis=-1, keepdims=True)
    eg = jnp.exp(gam)
    vals = (beta, gam, eg, beta * eg, jnp.exp(tot - gam), jnp.broadcast_to(jnp.exp(tot), gam.shape))
    for qi, v in enumerate(vals):
        o_ref[0, :, :, qi] = v.reshape(ng, nc, 2 * hb, LANES)


def _gates(g, alog_rows, dtb_rows, *, ng, nc, hb):
    b, _, rows, _ = g.shape
    return pl.pallas_call(
        functools.partial(_gate_kernel, hb=hb),
        grid=(b,),
        in_specs=[pl.BlockSpec((1, 2, rows, LANES), lambda i: (i, 0, 0, 0)),
                  pl.BlockSpec((rows, 1), lambda i: (0, 0)),
                  pl.BlockSpec((rows, 1), lambda i: (0, 0))],
        out_specs=pl.BlockSpec((1, ng, nc, 6, 2 * hb, LANES), lambda i: (i, 0, 0, 0, 0, 0)),
        out_shape=jax.ShapeDtypeStruct((b, ng, nc, 6, 2 * hb, LANES), F32),
    )(g, alog_rows, dtb_rows)


def _attn_kernel(q_ref, kc_ref, vc_ref, kx_ref, vx_ref, o_ref):
    kc = kc_ref[...]
    kx = kx_ref[...]
    vc = jnp.concatenate([vc_ref[...], jnp.ones(vc_ref.shape, BF16)], axis=1)
    vx = jnp.concatenate([vx_ref[...], jnp.ones(vx_ref.shape, BF16)], axis=1)
    tq = q_ref.shape[0]
    rq = min(tq, 2 * ROW_BLOCK)
    nblk = tq // rq

    def unit(u):
        hh, blk = divmod(u, nblk)
        return slice(blk * rq, (blk + 1) * rq), slice(hh * LANES, (hh + 1) * LANES)

    def scores(u):
        rows, cols = unit(u)
        q = q_ref[rows, cols]
        return _dot_nt(q, kc), _dot_nt(q, kx)

    def softmax_pv(u, s):
        rows, cols = unit(u)
        sc, sx = s
        m = jnp.maximum(jnp.max(sc, axis=-1, keepdims=True), jnp.max(sx, axis=-1, keepdims=True))
        pc = jnp.exp2(sc - m).astype(BF16)
        px = jnp.exp2(sx - m).astype(BF16)
        o = _dot(pc, vc) + _dot(px, vx)
        o_ref[rows, cols] = (o[:, :LANES] / o[:, LANES:LANES + 1]).astype(BF16)

    _skewed(ATTN_GROUP * nblk, scores, softmax_pv)


def _attention(p_lat, p_ctx, *, b, n, cl, tq):
    gw = ATTN_GROUP * LANES
    nq = n // tq
    q0 = COL_AQ // gw
    kx0, vx0 = COL_AK // LANES, COL_AV // LANES
    return pl.pallas_call(
        _attn_kernel,
        grid=(b, ATTN_KV_HEADS, nq),
        in_specs=[
            pl.BlockSpec((tq, gw), lambda i, h, t: (i * nq + t, q0 + h)),
            pl.BlockSpec((cl, LANES), lambda i, h, t: (i, kx0 + h)),
            pl.BlockSpec((cl, LANES), lambda i, h, t: (i, vx0 + h)),
            pl.BlockSpec((n, LANES), lambda i, h, t: (i, kx0 + h)),
            pl.BlockSpec((n, LANES), lambda i, h, t: (i, vx0 + h)),
        ],
        out_specs=pl.BlockSpec((tq, gw), lambda i, h, t: (i * nq + t, h)),
        out_shape=jax.ShapeDtypeStruct((b * n, ATTN_HEADS * LANES), BF16),
        compiler_params=pltpu.CompilerParams(dimension_semantics=("arbitrary", "arbitrary", "arbitrary"),
                                             vmem_limit_bytes=VMEM_LIMIT),
    )(p_lat, p_ctx, p_ctx, p_lat, p_lat)


G_BETA, G_GAM, G_EG, G_BEG, G_EGD, G_GL = range(6)


def _gdn_kernel(ql_ref, kl_ref, vl_ref, kc_ref, vc_ref, g_ref, of_ref, ob_ref, mq_ref, bo_ref, s_ref,
                *, hb, ncc, ncl, cu):
    c_sz = GDN_CHUNK
    ri = lax.broadcasted_iota(jnp.int32, (c_sz, c_sz), 0)
    ci = lax.broadcasted_iota(jnp.int32, (c_sz, c_sz), 1)
    eye = (ri == ci).astype(F32)
    levels = int(math.log2(c_sz))
    cu_ctx = math.gcd(ncc, cu)
    cu_lat = math.gcd(ncl, cu)

    def tri_masks(d):
        lo, hi = (ci, ri) if d == 0 else (ri, ci)
        incl = hi >= lo
        strict = hi > lo
        pair = []
        for lv in range(levels):
            bh = hi >> lv
            bl = lo >> lv
            pair.append(((bh & 1) == 1) & (bl == bh - 1))
        return incl, strict, pair

    masks = [tri_masks(0), tri_masks(1)]

    def grow(gt, qi, d, hh):
        return gt[qi, d * hb + hh:d * hb + hh + 1, :]

    def prep(chunks):
        inst = []
        for c, k, v, q in chunks:
            rq = pl.multiple_of(c * (2 * c_sz), 2 * c_sz)
            gt = g_ref[0, 0, c]
            gam_rows = gt[G_GAM]
            gam_cols = jnp.concatenate([gam_rows, jnp.zeros((c_sz - 2 * hb, c_sz), F32)], axis=0).T
            for hh in range(hb):
                kh = k[hh]
                kk = _dot_nt(kh, kh)
                qk = None if q is None else _dot_nt(q[hh], kh)
                kt = kh.astype(F32).T
                for d in range(2):
                    inst.append(dict(rq=rq, gt=gt, hh=hh, d=d, hd=hh * 2 + d, kh=kh, vh=v[hh], kk=kk, qk=qk, kt=kt,
                                     qh=None if q is None else q[hh],
                                     gam_c=gam_cols[:, d * hb + hh:d * hb + hh + 1]))
        for it in inst:
            d, hh, gt = it["d"], it["hh"], it["gt"]
            incl, strict, pair = masks[d]
            beta_r = grow(gt, G_BETA, d, hh)
            e = jnp.exp(jnp.where(incl, it["gam_c"] - grow(gt, G_GAM, d, hh), NEG_BIG))
            l2 = jnp.where(strict, e * it["kk"], 0.0) * beta_r
            it["t"] = eye - jnp.where(pair[0], l2, 0.0)
            it["ms"] = [jnp.where(pair[lv], l2, 0.0).astype(BF16) for lv in range(1, levels)]
            it["kb"] = (it["kt"] * (grow(gt, G_EGD, d, hh) * beta_r)).astype(BF16)
            if it["qk"] is not None:
                pb = (e * it["qk"] * beta_r).astype(BF16)
                it["pb"] = pb
                it["qlhs"] = jnp.concatenate([(eye * grow(gt, G_EG, d, hh)).astype(BF16), -pb], axis=1)
        for lv in range(levels - 1):
            xs = [_dot(it["ms"][lv], it["t"].astype(BF16)).astype(BF16) for it in inst]
            for it, xm in zip(inst, xs):
                it["t"] = it["t"] - _dot(it["t"].astype(BF16), xm)
        for it in inst:
            it["yu"] = _dot(it["t"].astype(BF16), it["vh"]).astype(BF16)
        for it in inst:
            it["yw"] = _dot((it["t"] * grow(it["gt"], G_EG, it["d"], it["hh"])).astype(BF16), it["kh"]).astype(BF16)
        for it in inst:
            bm = _dot(it["kb"], jnp.concatenate([it["yu"], it["yw"]], axis=1))
            mq_ref[it["hd"], pl.ds(it["rq"], c_sz), :] = (-bm[:, GDN_DV:]).astype(BF16)
            bo_ref[it["hd"], pl.ds(it["rq"], c_sz), :] = bm[:, :GDN_DV].astype(BF16)
        for it in inst:
            if it["qk"] is not None:
                oo = _dot(it["pb"], it["yu"])
                qq = _dot(it["qlhs"], jnp.concatenate([it["qh"], it["yw"]], axis=0))
                mq_ref[it["hd"], pl.ds(it["rq"] + c_sz, c_sz), :] = qq.astype(BF16)
                bo_ref[it["hd"], pl.ds(it["rq"] + c_sz, c_sz), :] = oo.astype(BF16)

    def split_heads(ref, r0):
        return [ref[pl.ds(r0, c_sz), hh * LANES:(hh + 1) * LANES] for hh in range(hb)]

    def prep_ctx(i, carry):
        chunks = []
        for j in range(cu_ctx):
            c = i * cu_ctx + j
            r0 = pl.multiple_of(c * c_sz, c_sz)
            chunks.append((c, split_heads(kc_ref, r0), split_heads(vc_ref, r0), None))
        prep(chunks)
        return carry

    def prep_lat(i, carry):
        chunks = []
        for j in range(cu_lat):
            c = i * cu_lat + j
            r0 = pl.multiple_of(c * c_sz, c_sz)
            chunks.append((c + ncc, split_heads(kl_ref, r0), split_heads(vl_ref, r0), split_heads(ql_ref, r0)))
        prep(chunks)
        return carry

    lax.fori_loop(0, ncc // cu_ctx, prep_ctx, 0)
    lax.fori_loop(0, ncl // cu_lat, prep_lat, 0)

    s_ref[...] = jnp.zeros_like(s_ref)

    def scan_steps(steps):
        st = []
        for hh, d, c, out_row in steps:
            hd = hh * 2 + d
            rq = pl.multiple_of(c * (2 * c_sz), 2 * c_sz)
            rows = c_sz if out_row is None else 2 * c_sz
            s = s_ref[hd]
            st.append(dict(hd=hd, hh=hh, d=d, s=s, rq=rq, rows=rows, out_row=out_row,
                           gl=g_ref[0, 0, c, G_GL, pl.ds(d * hb + hh, 1), :],
                           x=_dot(mq_ref[hd, pl.ds(rq, rows), :], s.astype(BF16))))
        for it in st:
            bo = bo_ref[it["hd"], pl.ds(it["rq"], it["rows"]), :].astype(F32)
            s_ref[it["hd"]] = it["gl"] * it["s"] + it["x"][:c_sz] + bo[:c_sz]
            if it["out_row"] is not None:
                out_ref = of_ref if it["d"] == 0 else ob_ref
                out_ref[pl.ds(it["out_row"], c_sz), it["hh"] * LANES:(it["hh"] + 1) * LANES] = (
                    it["x"][c_sz:] + bo[c_sz:]).astype(BF16)

    def scan_ctx(i, carry):
        steps = []
        for hh in range(hb):
            steps += [(hh, 0, i, None), (hh, 1, ncc - 1 - i, None)]
        scan_steps(steps)
        return carry

    def scan_lat(i, carry):
        steps = []
        for hh in range(hb):
            steps += [(hh, 0, ncc + i, pl.multiple_of(i * c_sz, c_sz)),
                      (hh, 1, ncc + ncl - 1 - i, pl.multiple_of((ncl - 1 - i) * c_sz, c_sz))]
        scan_steps(steps)
        return carry

    lax.fori_loop(0, ncc, scan_ctx, 0)
    lax.fori_loop(0, ncl, scan_lat, 0)


def _gdn(p_lat, p_ctx, gpack, *, b, n, cl, hb, cu):
    gw = hb * LANES
    ng = GDN_HEADS // hb
    ncc, ncl = cl // GDN_CHUNK, n // GDN_CHUNK
    nc = ncc + ncl
    q0, k0, v0 = COL_GQ // gw, COL_GK // gw, COL_GV // gw
    kern = functools.partial(_gdn_kernel, hb=hb, ncc=ncc, ncl=ncl, cu=cu)
    out = jax.ShapeDtypeStruct((b * n, GDN_HEADS * LANES), BF16)
    return pl.pallas_call(
        kern,
        grid=(b, ng),
        in_specs=[
            pl.BlockSpec((n, gw), lambda i, g: (i, q0 + g)),
            pl.BlockSpec((n, gw), lambda i, g: (i, k0 + g)),
            pl.BlockSpec((n, gw), lambda i, g: (i, v0 + g)),
            pl.BlockSpec((cl, gw), lambda i, g: (i, k0 + g)),
            pl.BlockSpec((cl, gw), lambda i, g: (i, v0 + g)),
            pl.BlockSpec((1, 1) + gpack.shape[2:], lambda i, g: (i, g, 0, 0, 0, 0)),
        ],
        out_specs=[pl.BlockSpec((n, gw), lambda i, g: (i, g)), pl.BlockSpec((n, gw), lambda i, g: (i, g))],
        out_shape=[out, out],
        scratch_shapes=[
            pltpu.VMEM((2 * hb, 2 * nc * GDN_CHUNK, LANES), BF16),
            pltpu.VMEM((2 * hb, 2 * nc * GDN_CHUNK, LANES), BF16),
            pltpu.VMEM((2 * hb, GDN_DK, GDN_DV), F32),
        ],
        compiler_params=pltpu.CompilerParams(dimension_semantics=("arbitrary", "arbitrary"),
                                             vmem_limit_bytes=VMEM_LIMIT),
    )(p_lat, p_lat, p_lat, p_ctx, p_ctx, gpack)


def _merge_kernel(x_ref, a_ref, of_ref, ob_ref, z0_ref, z1_ref, ga0_ref, ga1_ref, gd0_ref, gd1_ref, g1_ref, nw_ref,
                  wpa_ref, wpd_ref, wout_ref, o_ref):
    ya = _dot(a_ref[...], wpa_ref[...])
    heads = []
    per_half = GDN_HEADS // 2
    for hh in range(GDN_HEADS):
        cols = slice(hh * LANES, (hh + 1) * LANES)
        zcols = slice((hh % per_half) * LANES, (hh % per_half + 1) * LANES)
        z = (z0_ref if hh < per_half else z1_ref)[:, zcols].astype(F32)
        o = of_ref[:, cols].astype(F32) + ob_ref[:, cols].astype(F32)
        o = o * lax.rsqrt(jnp.mean(o * o, axis=-1, keepdims=True) + NORM_EPS) * nw_ref[...]
        heads.append((o * _silu(z)).astype(BF16))
    yd = _dot(jnp.concatenate(heads, axis=1), wpd_ref[...])
    ga = jnp.concatenate([ga0_ref[...], ga1_ref[...]], axis=1).astype(F32)
    gd = jnp.concatenate([gd0_ref[...], gd1_ref[...]], axis=1).astype(F32)
    y = ga * ya + gd * yd
    o_ref[...] = x_ref[...] + g1_ref[0] * _dot(y.astype(BF16), wout_ref[...])


def _merge(x2, attn, o_f, o_b, p_lat, g1, nw, wpa, wpd, wout, *, n, tm):
    m, d = x2.shape
    per_seq = n // tm
    row = lambda i: (i, 0)
    const = lambda i: (0, 0)
    return pl.pallas_call(
        _merge_kernel,
        grid=(m // tm,),
        in_specs=[
            pl.BlockSpec((tm, d), row),
            pl.BlockSpec((tm, d), row),
            pl.BlockSpec((tm, d), row),
            pl.BlockSpec((tm, d), row),
        ] + [pl.BlockSpec((tm, PROJ_TILE), functools.partial(lambda i, t: (i, t), t=c0 // PROJ_TILE + k))
             for c0 in (COL_Z, COL_GATES, COL_GATES + d) for k in range(2)] + [
            pl.BlockSpec((1, 1, d), lambda i: (i // per_seq, 0, 0)),
            pl.BlockSpec((1, LANES), const),
            pl.BlockSpec((d, d), const),
            pl.BlockSpec((d, d), const),
            pl.BlockSpec((d, d), const),
        ],
        out_specs=pl.BlockSpec((tm, d), row),
        out_shape=jax.ShapeDtypeStruct((m, d), F32),
        compiler_params=pltpu.CompilerParams(dimension_semantics=("arbitrary",), vmem_limit_bytes=VMEM_LIMIT),
    )(x2, attn, o_f, o_b, *([p_lat] * 6), g1, nw, wpa, wpd, wout)


def _ffn_kernel(x_ref, sh_ref, sc_ref, g2_ref, wg_ref, wv_ref, cwg_ref, cwv_ref, cbg_ref, cbv_ref, wd_ref, fw_ref,
                o_ref, h_ref, *, seq_len, rb):
    j = pl.program_id(1)
    tm = h_ref.shape[0]
    assert tm % rb == 0

    @pl.when(j == 0)
    def _():
        x = x_ref[...]
        r = lax.rsqrt(jnp.mean(x * x, axis=-1, keepdims=True) + NORM_EPS)
        h_ref[...] = ((x * r) * (1.0 + sc_ref[0]) + sh_ref[0]).astype(BF16)
        o_ref[...] = jnp.zeros_like(o_ref)

    wg = wg_ref[...].astype(BF16)
    wv = wv_ref[...].astype(BF16)
    wd = wd_ref[...].astype(BF16)

    def produce(r):
        _, lo, hi = _block_rows(r, rb, tm, seq_len)
        h = h_ref[lo:hi, :]
        return _dot(h, wg), _dot(h, wv)

    def consume(r, up):
        r0, lo, _ = _block_rows(r, rb, tm, seq_len)
        ug = _conv3_block(up[0], cwg_ref, r0, lo, seq_len, rb) + cbg_ref[...]
        uv = _conv3_block(up[1], cwv_ref, r0, lo, seq_len, rb) + cbv_ref[...]
        o_ref[r0:r0 + rb, :] += _dot((_silu(ug) * uv).astype(BF16), wd)

    _skewed(tm // rb, produce, consume)

    @pl.when(j == pl.num_programs(1) - 1)
    def _():
        y = x_ref[...] + g2_ref[0] * o_ref[...]
        o_ref[...] = y * lax.rsqrt(jnp.mean(y * y, axis=-1, keepdims=True) + NORM_EPS) * fw_ref[...]


def _ffn(x1, sh2, sc2, g2, wup, cw, cb, wdown, fw, *, n, tf, rb):
    m, d = x1.shape
    dff = wdown.shape[0]
    nj = dff // tf
    mod_map = lambda i, j: (i, 0, 0)
    return pl.pallas_call(
        functools.partial(_ffn_kernel, seq_len=n, rb=min(rb, n)),
        grid=(m // n, nj),
        in_specs=[
            pl.BlockSpec((n, d), lambda i, j: (i, 0)),
            pl.BlockSpec((1, 1, d), mod_map),
            pl.BlockSpec((1, 1, d), mod_map),
            pl.BlockSpec((1, 1, d), mod_map),
            pl.BlockSpec((d, tf), lambda i, j: (0, j)),
            pl.BlockSpec((d, tf), lambda i, j: (0, j + nj)),
            pl.BlockSpec((3, tf), lambda i, j: (0, j)),
            pl.BlockSpec((3, tf), lambda i, j: (0, j + nj)),
            pl.BlockSpec((1, tf), lambda i, j: (0, j)),
            pl.BlockSpec((1, tf), lambda i, j: (0, j + nj)),
            pl.BlockSpec((tf, d), lambda i, j: (j, 0)),
            pl.BlockSpec((1, d), lambda i, j: (0, 0)),
        ],
        out_specs=pl.BlockSpec((n, d), lambda i, j: (i, 0)),
        out_shape=jax.ShapeDtypeStruct((m, d), F32),
        scratch_shapes=[pltpu.VMEM((n, d), BF16)],
        compiler_params=pltpu.CompilerParams(dimension_semantics=("arbitrary", "arbitrary"),
                                             vmem_limit_bytes=VMEM_LIMIT),
    )(x1, sh2, sc2, g2, wup, wup, cw, cw, cb, cb, wdown, fw)


def _rope_tables(n):
    rows = n // GRID_W
    row_ids = jnp.broadcast_to(jnp.arange(rows, dtype=F32)[:, None], (rows, GRID_W)).reshape(n)
    col_ids = jnp.broadcast_to(jnp.arange(GRID_W, dtype=F32)[None, :], (rows, GRID_W)).reshape(n)
    inv_freq = ROPE_THETA ** (-jnp.arange(0, ROPE_AXIS_DIM, 2, dtype=F32) / ROPE_AXIS_DIM)
    ar = row_ids[:, None] * inv_freq
    ac = col_ids[:, None] * inv_freq
    cos = jnp.concatenate([jnp.cos(ar), jnp.cos(ar), jnp.cos(ac), jnp.cos(ac)], axis=-1)
    sin = jnp.concatenate([-jnp.sin(ar), jnp.sin(ar), -jnp.sin(ac), jnp.sin(ac)], axis=-1)
    return cos, sin


def _gained_rope_tables(cos, sin, gain, scale):
    swapped_gain = gain.reshape(-1, 2, ROPE_AXIS_DIM // 2)[:, ::-1].reshape(gain.shape)
    return cos * (gain * scale), sin * (swapped_gain * scale)


def kernel(x, c, ctx, c_ctx, w_mod, b_mod, w_in, q_norm_w, k_norm_w, conv_qkv_w, a_log, dt_bias, gdn_norm_w, w_pa,
           w_pd, w_out, w_up, ffn_conv_w, ffn_conv_b, w_down, final_norm_w):
    assert w_mod.shape[0] == 1, "single-layer block"
    b, n, d = x.shape
    cl = ctx.shape[1]
    hb = 4

    pad = (-(b + 1)) % 8
    cc = jnp.concatenate([c, c_ctx[None, :], jnp.zeros((pad, d), F32)], axis=0)
    mod = _modulation(cc, w_mod[0], b_mod[0][None, :])
    sh1, sc1, g1, sh2, sc2, g2 = [t[:, None, :] for t in jnp.split(mod[:b], 6, axis=-1)]
    csh1, csc1 = mod[b:b + 1, None, :d], mod[b:b + 1, None, d:2 * d]

    w = w_in[0]
    akv, qkv_w = ATTN_KV_HEADS * HEAD_DIM, 3 * GDN_HEADS * GDN_DK
    o_qkv = 2 * akv
    o_db = o_qkv + qkv_w
    o_aq = o_db + 4 * GDN_HEADS
    w16 = w.astype(BF16)
    w_behind = w16[:, o_aq:]
    w_small = jnp.pad(w16[:, o_db:o_aq], ((0, 0), (0, LANES - 4 * GDN_HEADS)))
    cos, sin = _rope_tables(n)
    qn, kn = q_norm_w, k_norm_w
    tables = _gained_rope_tables(cos, sin, qn[0], SCORE_SCALE) + _gained_rope_tables(cos, sin, kn[0], 1.0)
    x2 = x.reshape(b * n, d)
    common = (w16, w_behind, w_small, qn, kn, conv_qkv_w[0], tables)
    p_lat, s_lat = _input_projection(x2, sh1, sc1, *common, tm=n, seq_len=n, tile0=0,
                                     ntiles=MAIN_COLS // PROJ_TILE, rope=True)
    p_ctx, s_ctx = _input_projection(ctx.reshape(b * cl, d), csh1, csc1, *common, tm=b * cl, seq_len=cl,
                                     tile0=0, ntiles=CTX_COLS // PROJ_TILE, rope=False)

    ncc, ncl = cl // GDN_CHUNK, n // GDN_CHUNK
    nc = ncc + ncl
    ng = GDN_HEADS // hb
    nsm = 4 * GDN_HEADS
    small = jnp.concatenate([s_ctx[:, :nsm].reshape(b, cl, nsm), s_lat[:, :nsm].reshape(b, n, nsm)], axis=1)
    small = small.reshape(b, nc, GDN_CHUNK, 2, 2, ng, hb).transpose(0, 3, 5, 1, 4, 6, 2)
    small = small.reshape(b, 2, ng * nc * 2 * hb, GDN_CHUNK)

    def per_row(p):
        p = jnp.broadcast_to(p.reshape(2, ng, 1, hb).transpose(1, 2, 0, 3), (ng, nc, 2, hb))
        return p.reshape(-1, 1)

    gpack = _gates(small, per_row(a_log[0]), per_row(dt_bias[0]), ng=ng, nc=nc, hb=hb)

    attn = _attention(p_lat, p_ctx, b=b, n=n, cl=cl, tq=min(n, 1024))
    o_f, o_b = _gdn(p_lat, p_ctx, gpack, b=b, n=n, cl=cl, hb=hb, cu=4)

    x1 = _merge(x2, attn, o_f, o_b, p_lat, g1, gdn_norm_w, w_pa[0].astype(BF16), w_pd[0].astype(BF16),
                w_out[0].astype(BF16), n=n, tm=min(n, 512))

    out = _ffn(x1, sh2, sc2, g2, w_up[0], ffn_conv_w[0], ffn_conv_b[0][None, :], w_down[0], final_norm_w[None, :],
               n=n, tf=256, rb=4 * ROW_BLOCK)
    return out.reshape(b, n, d)
```

```python
import functools
import math

import jax
import jax.numpy as jnp
from jax import lax
from jax.experimental import pallas as pl
from jax.experimental.pallas import tpu as pltpu

F32 = jnp.float32
BF16 = jnp.bfloat16

LANES = 128
GRID_W = 64
ATTN_HEADS = 8
ATTN_KV_HEADS = 2
ATTN_GROUP = ATTN_HEADS // ATTN_KV_HEADS
HEAD_DIM = 128
ROPE_AXIS_DIM = HEAD_DIM // 2
ROPE_THETA = 10000.0
GDN_HEADS = 8
GDN_DK = 128
GDN_DV = 128
GDN_CHUNK = 128
NORM_EPS = 1e-6
SCORE_SCALE = HEAD_DIM ** -0.5 * math.log2(math.e)
NEG_BIG = -1e30

PROJ_TILE = 512
UNIT_COLS = 256
COL_AK = 0
COL_AV = 256
COL_GQ = 512
COL_GK = 1536
COL_GV = 2560
COL_AQ = 3584
COL_Z = 4608
COL_GATES = 5632
MAIN_COLS = 7680
CTX_COLS = COL_AQ
VMEM_LIMIT = 56 * 1024 * 1024


def _dot(a, b):
    return jnp.dot(a, b, preferred_element_type=F32)


def _dot_nt(a, b):
    return lax.dot_general(a, b, (((1,), (1,)), ((), ())), preferred_element_type=F32)


def _sigmoid(x):
    return 0.5 * jnp.tanh(0.5 * x) + 0.5


def _silu(x):
    h = 0.5 * x
    return h * jnp.tanh(h) + h


def _mod_kernel(c_ref, w_ref, b_ref, o_ref):
    s = _silu(c_ref[...])
    w = w_ref[...]
    s_hi = s.astype(BF16)
    s_lo = (s - s_hi.astype(F32)).astype(BF16)
    w_hi = w.astype(BF16)
    w_lo = (w - w_hi.astype(F32)).astype(BF16)
    o_ref[...] = _dot(s_hi, w_hi) + _dot(s_lo, w_hi) + _dot(s_hi, w_lo) + b_ref[...]


def _modulation(cc, w_mod, b_mod):
    rows, d = cc.shape
    n = w_mod.shape[1]
    tn = 1024
    return pl.pallas_call(
        _mod_kernel,
        grid=(n // tn,),
        in_specs=[pl.BlockSpec((rows, d), lambda j: (0, 0)),
                  pl.BlockSpec((d, tn), lambda j: (0, j)),
                  pl.BlockSpec((1, tn), lambda j: (0, j))],
        out_specs=pl.BlockSpec((rows, tn), lambda j: (0, j)),
        out_shape=jax.ShapeDtypeStruct((rows, n), F32),
    )(cc, w_mod, b_mod)


ROW_BLOCK = 256
ROW_HALO = 16


def _skewed(n, produce, consume):
    prev = produce(0)
    for r in range(1, n):
        cur = produce(r)
        consume(r - 1, prev)
        prev = cur
    consume(n - 1, prev)


def _block_rows(r, rb, tm, seq_len):
    r0 = r * rb
    halo = ROW_HALO if seq_len > rb else 0
    return r0, max(r0 - halo, 0), min(r0 + rb + halo, tm)


def _conv3_block(ext, w_ref, r0, lo, seq_len, rb):
    assert seq_len % rb == 0
    n_ext = ext.shape[0]
    off = r0 - lo
    prev = pltpu.roll(ext, 1, axis=0)[off:off + rb]
    nxt = pltpu.roll(ext, n_ext - 1, axis=0)[off:off + rb]
    row = lax.broadcasted_iota(jnp.int32, (rb, 1), 0)
    if r0 % seq_len == 0:
        prev = jnp.where(row == 0, 0.0, prev)
    if (r0 + rb) % seq_len == 0:
        nxt = jnp.where(row == rb - 1, 0.0, nxt)
    return prev * w_ref[0:1, :] + ext[off:off + rb] * w_ref[1:2, :] + nxt * w_ref[2:3, :]


def _rope(a, cos, sin):
    lane = lax.broadcasted_iota(jnp.int32, a.shape, 1)
    swapped = jnp.where((lane & 32) == 0, pltpu.roll(a, 96, axis=1), pltpu.roll(a, 32, axis=1))
    return a * cos + swapped * sin


def _inproj_kernel(x_ref, sh_ref, sc_ref, wa_ref, wb_ref, ws_ref, qn_ref, kn_ref, cw_ref, qc_ref, qs_ref, kc_ref, ks_ref,
                   o_ref, os_ref, hx_ref, *, tile0, ntiles, seq_len, rope, rb):
    j = pl.program_id(1)
    tm = hx_ref.shape[0]
    assert tm % rb == 0

    @pl.when(j == 0)
    def _():
        x = x_ref[...]
        r = lax.rsqrt(jnp.mean(x * x, axis=-1, keepdims=True) + NORM_EPS)
        h = (x * r) * (1.0 + sc_ref[0]) + sh_ref[0]
        hb = h.astype(BF16)
        hx_ref[...] = hb
        os_ref[...] = _dot(hb, ws_ref[...])

    kind = j + tile0
    nh = UNIT_COLS // LANES

    def head(a, hh):
        return a[:, hh * LANES:(hh + 1) * LANES]

    def store(r0, c0, a):
        o_ref[r0:r0 + rb, c0:c0 + a.shape[1]] = a.astype(BF16)

    def qk_norm_rope(a, w_ref, cos_ref, sin_ref, scale, r0):
        r = lax.rsqrt(jnp.mean(a * a, axis=-1, keepdims=True) + NORM_EPS)
        if rope:
            return r * _rope(a, cos_ref[r0:r0 + rb, :], sin_ref[r0:r0 + rb, :])
        return a * (r * scale) * w_ref[...]

    def ep_gates(acc, r0, lo, c0):
        store(r0, c0, _sigmoid(acc))

    def ep_attn_q(acc, r0, lo, c0):
        for hh in range(nh):
            store(r0, c0 + hh * LANES, qk_norm_rope(head(acc, hh), qn_ref, qc_ref, qs_ref, SCORE_SCALE, r0))

    def ep_plain(acc, r0, lo, c0):
        store(r0, c0, acc)

    def conv_silu(ext, r0, lo, c0):
        return _silu(_conv3_block(ext, cw_ref.at[:, c0:c0 + UNIT_COLS], r0, lo, seq_len, rb))

    def ep_gdn_qk(scale):
        def ep(ext, r0, lo, c0):
            y = conv_silu(ext, r0, lo, c0)
            for hh in range(nh):
                a = head(y, hh)
                store(r0, c0 + hh * LANES, a * (lax.rsqrt(jnp.sum(a * a, axis=-1, keepdims=True) + NORM_EPS) * scale))
        return ep

    def ep_gdn_v(ext, r0, lo, c0):
        store(r0, c0, conv_silu(ext, r0, lo, c0))

    def ep_attn_kv(acc, r0, lo, c0):
        if c0 >= ATTN_KV_HEADS * LANES:
            return store(r0, c0, acc)
        for hh in range(nh):
            store(r0, c0 + hh * LANES, qk_norm_rope(head(acc, hh), kn_ref, kc_ref, ks_ref, 1.0, r0))

    bounds = [c // PROJ_TILE for c in (COL_AK, COL_GQ, COL_GK, COL_GV, COL_AQ, COL_Z, COL_GATES, MAIN_COLS)]
    eps = [(ep_attn_kv, False), (ep_gdn_qk(GDN_DK ** -0.5), True), (ep_gdn_qk(1.0), True), (ep_gdn_v, True),
           (ep_attn_q, False), (ep_plain, False), (ep_gates, False)]
    groups = [(lo, hi) + ep for lo, hi, ep in zip(bounds[:-1], bounds[1:], eps)]
    ncol = PROJ_TILE // UNIT_COLS

    for k_lo, k_hi, epilogue, conv in groups:
        if max(k_lo, tile0) >= min(k_hi, tile0 + ntiles):
            continue

        w_ref = wa_ref if k_lo < CTX_COLS // PROJ_TILE else wb_ref

        def run(epilogue=epilogue, conv=conv, w_ref=w_ref):
            def rows(u):
                r = u // ncol
                return _block_rows(r, rb, tm, seq_len) if conv else (r * rb, r * rb, (r + 1) * rb)

            def produce(u):
                _, lo, hi = rows(u)
                c0 = (u % ncol) * UNIT_COLS
                return _dot(hx_ref[lo:hi, :], w_ref[:, c0:c0 + UNIT_COLS])

            def consume(u, acc):
                r0, lo, _ = rows(u)
                epilogue(acc, r0, lo, (u % ncol) * UNIT_COLS)

            _skewed((tm // rb) * ncol, produce, consume)

        pl.when((kind >= k_lo) & (kind < k_hi))(run)


def _input_projection(x2, shift, scale, w_a, w_b, w_small, qn, kn, cw, tables, *, tm, seq_len, tile0, ntiles, rope):
    m, d = x2.shape
    nrow = m // tm
    per_row_mod = shift.shape[0] > 1
    mod_map = (lambda i, j: (i, 0, 0)) if per_row_mod else (lambda i, j: (0, 0, 0))
    conv_tile0 = COL_GQ // PROJ_TILE
    n_conv_tiles = cw.shape[1] // PROJ_TILE
    na, nb = CTX_COLS // PROJ_TILE, w_b.shape[1] // PROJ_TILE
    kern = functools.partial(_inproj_kernel, tile0=tile0, ntiles=ntiles, seq_len=seq_len, rope=rope,
                             rb=min(2 * ROW_BLOCK, seq_len))
    return pl.pallas_call(
        kern,
        grid=(nrow, ntiles),
        in_specs=[
            pl.BlockSpec((tm, d), lambda i, j: (i, 0)),
            pl.BlockSpec((1, 1, d), mod_map),
            pl.BlockSpec((1, 1, d), mod_map),
            pl.BlockSpec((d, PROJ_TILE), lambda i, j: (0, jnp.minimum(j + tile0, na - 1))),
            pl.BlockSpec((d, PROJ_TILE), lambda i, j: (0, jnp.clip(j + tile0 - na, 0, nb - 1))),
            pl.BlockSpec((d, LANES), lambda i, j: (0, 0)),
            pl.BlockSpec((1, LANES), lambda i, j: (0, 0)),
            pl.BlockSpec((1, LANES), lambda i, j: (0, 0)),
            pl.BlockSpec((3, PROJ_TILE), lambda i, j: (0, jnp.clip(j + tile0 - conv_tile0, 0, n_conv_tiles - 1))),
        ] + [pl.BlockSpec(t.shape, lambda i, j: (0, 0), pipeline_mode=pl.Buffered(1)) for t in tables],
        out_specs=[pl.BlockSpec((tm, PROJ_TILE), lambda i, j: (i, j)),
                   pl.BlockSpec((tm, LANES), lambda i, j: (i, 0))],
        out_shape=[jax.ShapeDtypeStruct((m, ntiles * PROJ_TILE), BF16),
                   jax.ShapeDtypeStruct((m, LANES), F32)],
        scratch_shapes=[pltpu.VMEM((tm, d), BF16)],
        compiler_params=pltpu.CompilerParams(dimension_semantics=("arbitrary", "arbitrary"),
                                             vmem_limit_bytes=VMEM_LIMIT),
    )(x2, shift, scale, w_a, w_b, w_small, qn, kn, cw, *tables)


def _gate_kernel(g_ref, alog_ref, dtb_ref, o_ref, *, hb):
    ng, nc = o_ref.shape[1], o_ref.shape[2]
    assert 2 * hb == 8, "one (direction, head) group per sublane tile"
    db = g_ref[0, 0]
    da = g_ref[0, 1]
    beta = _sigmoid(db)
    z = da + dtb_ref[...]
    softplus = jnp.maximum(z, 0.0) + jnp.log(1.0 + jnp.exp(-jnp.abs(z)))
    la = -jnp.exp(alog_ref[...]) * softplus
    lane = lax.broadcasted_iota(jnp.int32, la.shape, 1)
    backward = (lax.broadcasted_iota(jnp.int32, la.shape, 0) // hb) % 2 == 1
    pre = la
    suf = la
    k = 1
    while k < LANES:
        pre = pre + jnp.where(lane >= k, pltpu.roll(pre, k, axis=1), 0.0)
        suf = suf + jnp.where(lane < LANES - k, pltpu.roll(suf, LANES - k, axis=1), 0.0)
        k *= 2
    gam = jnp.where(backward, suf, pre)
    tot = jnp.sum(la, axis=-1, keepdims=True)
    eg = jnp.exp(gam)
    vals = (beta, gam, eg, beta * eg, jnp.exp(tot - gam), jnp.broadcast_to(jnp.exp(tot), gam.shape))
    for qi, v in enumerate(vals):
        o_ref[0, :, :, qi] = v.reshape(ng, nc, 2 * hb, LANES)


def _gates(g, alog_rows, dtb_rows, *, ng, nc, hb):
    b, _, rows, _ = g.shape
    return pl.pallas_call(
        functools.partial(_gate_kernel, hb=hb),
        grid=(b,),
        in_specs=[pl.BlockSpec((1, 2, rows, LANES), lambda i: (i, 0, 0, 0)),
                  pl.BlockSpec((rows, 1), lambda i: (0, 0)),
                  pl.BlockSpec((rows, 1), lambda i: (0, 0))],
        out_specs=pl.BlockSpec((1, ng, nc, 6, 2 * hb, LANES), lambda i: (i, 0, 0, 0, 0, 0)),
        out_shape=jax.ShapeDtypeStruct((b, ng, nc, 6, 2 * hb, LANES), F32),
    )(g, alog_rows, dtb_rows)


def _attn_kernel(q_ref, kc_ref, vc_ref, kx_ref, vx_ref, o_ref):
    kc = kc_ref[...]
    kx = kx_ref[...]
    vc = jnp.concatenate([vc_ref[...], jnp.ones(vc_ref.shape, BF16)], axis=1)
    vx = jnp.concatenate([vx_ref[...], jnp.ones(vx_ref.shape, BF16)], axis=1)
    tq = q_ref.shape[0]
    rq = min(tq, 2 * ROW_BLOCK)
    nblk = tq // rq

    def unit(u):
        hh, blk = divmod(u, nblk)
        return slice(blk * rq, (blk + 1) * rq), slice(hh * LANES, (hh + 1) * LANES)

    def scores(u):
        rows, cols = unit(u)
        q = q_ref[rows, cols]
        return _dot_nt(q, kc), _dot_nt(q, kx)

    def softmax_pv(u, s):
        rows, cols = unit(u)
        sc, sx = s
        m = jnp.maximum(jnp.max(sc, axis=-1, keepdims=True), jnp.max(sx, axis=-1, keepdims=True))
        pc = jnp.exp2(sc - m).astype(BF16)
        px = jnp.exp2(sx - m).astype(BF16)
        o = _dot(pc, vc) + _dot(px, vx)
        o_ref[rows, cols] = (o[:, :LANES] / o[:, LANES:LANES + 1]).astype(BF16)

    _skewed(ATTN_GROUP * nblk, scores, softmax_pv)


def _attention(p_lat, p_ctx, *, b, n, cl, tq):
    gw = ATTN_GROUP * LANES
    nq = n // tq
    q0 = COL_AQ // gw
    kx0, vx0 = COL_AK // LANES, COL_AV // LANES
    return pl.pallas_call(
        _attn_kernel,
        grid=(b, ATTN_KV_HEADS, nq),
        in_specs=[
            pl.BlockSpec((tq, gw), lambda i, h, t: (i * nq + t, q0 + h)),
            pl.BlockSpec((cl, LANES), lambda i, h, t: (i, kx0 + h)),
            pl.BlockSpec((cl, LANES), lambda i, h, t: (i, vx0 + h)),
            pl.BlockSpec((n, LANES), lambda i, h, t: (i, kx0 + h)),
            pl.BlockSpec((n, LANES), lambda i, h, t: (i, vx0 + h)),
        ],
        out_specs=pl.BlockSpec((tq, gw), lambda i, h, t: (i * nq + t, h)),
        out_shape=jax.ShapeDtypeStruct((b * n, ATTN_HEADS * LANES), BF16),
        compiler_params=pltpu.CompilerParams(dimension_semantics=("arbitrary", "arbitrary", "arbitrary"),
                                             vmem_limit_bytes=VMEM_LIMIT),
    )(p_lat, p_ctx, p_ctx, p_lat, p_lat)


G_BETA, G_GAM, G_EG, G_BEG, G_EGD, G_GL = range(6)


def _gdn_kernel(ql_ref, kl_ref, vl_ref, kc_ref, vc_ref, g_ref, of_ref, ob_ref, mq_ref, bo_ref, s_ref,
                *, hb, ncc, ncl, cu):
    c_sz = GDN_CHUNK
    ri = lax.broadcasted_iota(jnp.int32, (c_sz, c_sz), 0)
    ci = lax.broadcasted_iota(jnp.int32, (c_sz, c_sz), 1)
    eye = (ri == ci).astype(F32)
    levels = int(math.log2(c_sz))
    cu_ctx = math.gcd(ncc, cu)

    def tri_masks(d):
        lo, hi = (ci, ri) if d == 0 else (ri, ci)
        incl = hi >= lo
        strict = hi > lo
        pair = []
        for lv in range(levels):
            bh = hi >> lv
            bl = lo >> lv
            pair.append(((bh & 1) == 1) & (bl == bh - 1))
        return incl, strict, pair

    masks = [tri_masks(0), tri_masks(1)]

    def grow(gt, qi, d, hh):
        return gt[qi, d * hb + hh:d * hb + hh + 1, :]

    def prep(chunks):
        inst = []
        for c, k, v, q in chunks:
            rq = pl.multiple_of(c * (2 * c_sz), 2 * c_sz)
            gt = g_ref[0, 0, c]
            gam_rows = gt[G_GAM]
            gam_cols = jnp.concatenate([gam_rows, jnp.zeros((c_sz - 2 * hb, c_sz), F32)], axis=0).T
            for hh in range(hb):
                kh = k[hh]
                kk = _dot_nt(kh, kh)
                qk = None if q is None else _dot_nt(q[hh], kh)
                kt = kh.astype(F32).T
                for d in range(2):
                    inst.append(dict(rq=rq, gt=gt, hh=hh, d=d, hd=hh * 2 + d, kh=kh, vh=v[hh], kk=kk, qk=qk, kt=kt,
                                     qh=None if q is None else q[hh],
                                     gam_c=gam_cols[:, d * hb + hh:d * hb + hh + 1]))
        for it in inst:
            d, hh, gt = it["d"], it["hh"], it["gt"]
            incl, strict, pair = masks[d]
            beta_r = grow(gt, G_BETA, d, hh)
            e = jnp.exp(jnp.where(incl, it["gam_c"] - grow(gt, G_GAM, d, hh), NEG_BIG))
            l2 = jnp.where(strict, e * it["kk"], 0.0) * beta_r
            it["t"] = eye - jnp.where(pair[0], l2, 0.0)
            it["ms"] = [jnp.where(pair[lv], l2, 0.0).astype(BF16) for lv in range(1, levels)]
            it["kb"] = (it["kt"] * (grow(gt, G_EGD, d, hh) * beta_r)).astype(BF16)
            if it["qk"] is not None:
                pb = (e * it["qk"] * beta_r).astype(BF16)
                it["pb"] = pb
                it["qlhs"] = jnp.concatenate([(eye * grow(gt, G_EG, d, hh)).astype(BF16), -pb], axis=1)
        for lv in range(levels - 1):
            xs = [_dot(it["ms"][lv], it["t"].astype(BF16)).astype(BF16) for it in inst]
            for it, xm in zip(inst, xs):
                it["t"] = it["t"] - _dot(it["t"].astype(BF16), xm)
        for it in inst:
            it["yu"] = _dot(it["t"].astype(BF16), it["vh"]).astype(BF16)
        for it in inst:
            it["yw"] = _dot((it["t"] * grow(it["gt"], G_EG, it["d"], it["hh"])).astype(BF16), it["kh"]).astype(BF16)
        for it in inst:
            bm = _dot(it["kb"], jnp.concatenate([it["yu"], it["yw"]], axis=1))
            mq_ref[it["hd"], pl.ds(it["rq"], c_sz), :] = (-bm[:, GDN_DV:]).astype(BF16)
            bo_ref[it["hd"], pl.ds(it["rq"], c_sz), :] = bm[:, :GDN_DV].astype(BF16)
        for it in inst:
            if it["qk"] is not None:
                oo = _dot(it["pb"], it["yu"])
                qq = _dot(it["qlhs"], jnp.concatenate([it["qh"], it["yw"]], axis=0))
                mq_ref[it["hd"], pl.ds(it["rq"] + c_sz, c_sz), :] = qq.astype(BF16)
                bo_ref[it["hd"], pl.ds(it["rq"] + c_sz, c_sz), :] = oo.astype(BF16)

    def split_heads(ref, r0):
        return [ref[pl.ds(r0, c_sz), hh * LANES:(hh + 1) * LANES] for hh in range(hb)]

    def prep_ctx(i, carry):
        chunks = []
        for j in range(cu_ctx):
            c = i * cu_ctx + j
            r0 = pl.multiple_of(c * c_sz, c_sz)
            chunks.append((c, split_heads(kc_ref, r0), split_heads(vc_ref, r0), None))
        prep(chunks)
        return carry

    def prep_lat_pair(i):
        chunks = []
        for c in (i, ncl - 1 - i):
            r0 = pl.multiple_of(c * c_sz, c_sz)
            chunks.append((c + ncc, split_heads(kl_ref, r0), split_heads(vl_ref, r0), split_heads(ql_ref, r0)))
        prep(chunks)

    def scan_steps(steps):
        st = []
        for hh, d, c, out_row in steps:
            hd = hh * 2 + d
            rq = pl.multiple_of(c * (2 * c_sz), 2 * c_sz)
            rows = c_sz if out_row is None else 2 * c_sz
            s = s_ref[hd]
            st.append(dict(hd=hd, hh=hh, d=d, s=s, rq=rq, rows=rows, out_row=out_row,
                           gl=g_ref[0, 0, c, G_GL, pl.ds(d * hb + hh, 1), :],
                           x=_dot(mq_ref[hd, pl.ds(rq, rows), :], s.astype(BF16))))
        for it in st:
            bo = bo_ref[it["hd"], pl.ds(it["rq"], it["rows"]), :].astype(F32)
            s_ref[it["hd"]] = it["gl"] * it["s"] + it["x"][:c_sz] + bo[:c_sz]
            if it["out_row"] is not None:
                out_ref = of_ref if it["d"] == 0 else ob_ref
                out_ref[pl.ds(it["out_row"], c_sz), it["hh"] * LANES:(it["hh"] + 1) * LANES] = (
                    it["x"][c_sz:] + bo[c_sz:]).astype(BF16)

    def scan_ctx(i, carry):
        steps = []
        for hh in range(hb):
            steps += [(hh, 0, i, None), (hh, 1, ncc - 1 - i, None)]
        scan_steps(steps)
        return carry

    def scan_lat(i):
        steps = []
        for hh in range(hb):
            steps += [(hh, 0, ncc + i, pl.multiple_of(i * c_sz, c_sz)),
                      (hh, 1, ncc + ncl - 1 - i, pl.multiple_of((ncl - 1 - i) * c_sz, c_sz))]
        scan_steps(steps)

    assert ncl % 2 == 0
    lax.fori_loop(0, ncc // cu_ctx, prep_ctx, 0)
    s_ref[...] = jnp.zeros_like(s_ref)
    prep_lat_pair(0)
    lax.fori_loop(0, ncc, scan_ctx, 0)

    def scan_and_prep(i, carry):
        scan_lat(i)
        prep_lat_pair(i + 1)
        return carry

    def scan_only(i, carry):
        scan_lat(i)
        return carry

    lax.fori_loop(0, ncl // 2 - 1, scan_and_prep, 0)
    lax.fori_loop(ncl // 2 - 1, ncl, scan_only, 0)


def _gdn(p_lat, p_ctx, gpack, *, b, n, cl, hb, cu):
    gw = hb * LANES
    ng = GDN_HEADS // hb
    ncc, ncl = cl // GDN_CHUNK, n // GDN_CHUNK
    nc = ncc + ncl
    q0, k0, v0 = COL_GQ // gw, COL_GK // gw, COL_GV // gw
    kern = functools.partial(_gdn_kernel, hb=hb, ncc=ncc, ncl=ncl, cu=cu)
    out = jax.ShapeDtypeStruct((b * n, GDN_HEADS * LANES), BF16)
    return pl.pallas_call(
        kern,
        grid=(b, ng),
        in_specs=[
            pl.BlockSpec((n, gw), lambda i, g: (i, q0 + g)),
            pl.BlockSpec((n, gw), lambda i, g: (i, k0 + g)),
            pl.BlockSpec((n, gw), lambda i, g: (i, v0 + g)),
            pl.BlockSpec((cl, gw), lambda i, g: (i, k0 + g)),
            pl.BlockSpec((cl, gw), lambda i, g: (i, v0 + g)),
            pl.BlockSpec((1, 1) + gpack.shape[2:], lambda i, g: (i, g, 0, 0, 0, 0)),
        ],
        out_specs=[pl.BlockSpec((n, gw), lambda i, g: (i, g)), pl.BlockSpec((n, gw), lambda i, g: (i, g))],
        out_shape=[out, out],
        scratch_shapes=[
            pltpu.VMEM((2 * hb, 2 * nc * GDN_CHUNK, LANES), BF16),
            pltpu.VMEM((2 * hb, 2 * nc * GDN_CHUNK, LANES), BF16),
            pltpu.VMEM((2 * hb, GDN_DK, GDN_DV), F32),
        ],
        compiler_params=pltpu.CompilerParams(dimension_semantics=("arbitrary", "arbitrary"),
                                             vmem_limit_bytes=VMEM_LIMIT),
    )(p_lat, p_lat, p_lat, p_ctx, p_ctx, gpack)


def _merge_kernel(x_ref, a_ref, of_ref, ob_ref, z0_ref, z1_ref, ga0_ref, ga1_ref, gd0_ref, gd1_ref, g1_ref, nw_ref,
                  wpa_ref, wpd_ref, wout_ref, o_ref):
    ya = _dot(a_ref[...], wpa_ref[...])
    heads = []
    per_half = GDN_HEADS // 2
    for hh in range(GDN_HEADS):
        cols = slice(hh * LANES, (hh + 1) * LANES)
        zcols = slice((hh % per_half) * LANES, (hh % per_half + 1) * LANES)
        z = (z0_ref if hh < per_half else z1_ref)[:, zcols].astype(F32)
        o = of_ref[:, cols].astype(F32) + ob_ref[:, cols].astype(F32)
        o = o * lax.rsqrt(jnp.mean(o * o, axis=-1, keepdims=True) + NORM_EPS) * nw_ref[...]
        heads.append((o * _silu(z)).astype(BF16))
    yd = _dot(jnp.concatenate(heads, axis=1), wpd_ref[...])
    ga = jnp.concatenate([ga0_ref[...], ga1_ref[...]], axis=1).astype(F32)
    gd = jnp.concatenate([gd0_ref[...], gd1_ref[...]], axis=1).astype(F32)
    y = ga * ya + gd * yd
    o_ref[...] = x_ref[...] + g1_ref[0] * _dot(y.astype(BF16), wout_ref[...])


def _merge(x2, attn, o_f, o_b, p_lat, g1, nw, wpa, wpd, wout, *, n, tm):
    m, d = x2.shape
    per_seq = n // tm
    row = lambda i: (i, 0)
    const = lambda i: (0, 0)
    return pl.pallas_call(
        _merge_kernel,
        grid=(m // tm,),
        in_specs=[
            pl.BlockSpec((tm, d), row),
            pl.BlockSpec((tm, d), row),
            pl.BlockSpec((tm, d), row),
            pl.BlockSpec((tm, d), row),
        ] + [pl.BlockSpec((tm, PROJ_TILE), functools.partial(lambda i, t: (i, t), t=c0 // PROJ_TILE + k))
             for c0 in (COL_Z, COL_GATES, COL_GATES + d) for k in range(2)] + [
            pl.BlockSpec((1, 1, d), lambda i: (i // per_seq, 0, 0)),
            pl.BlockSpec((1, LANES), const),
            pl.BlockSpec((d, d), const),
            pl.BlockSpec((d, d), const),
            pl.BlockSpec((d, d), const),
        ],
        out_specs=pl.BlockSpec((tm, d), row),
        out_shape=jax.ShapeDtypeStruct((m, d), F32),
        compiler_params=pltpu.CompilerParams(dimension_semantics=("arbitrary",), vmem_limit_bytes=VMEM_LIMIT),
    )(x2, attn, o_f, o_b, *([p_lat] * 6), g1, nw, wpa, wpd, wout)


def _ffn_kernel(x_ref, sh_ref, sc_ref, g2_ref, wg_ref, wv_ref, cwg_ref, cwv_ref, cbg_ref, cbv_ref, wd_ref, fw_ref,
                o_ref, h_ref, *, seq_len, rb):
    j = pl.program_id(1)
    tm = h_ref.shape[0]
    assert tm % rb == 0

    @pl.when(j == 0)
    def _():
        x = x_ref[...]
        r = lax.rsqrt(jnp.mean(x * x, axis=-1, keepdims=True) + NORM_EPS)
        h_ref[...] = ((x * r) * (1.0 + sc_ref[0]) + sh_ref[0]).astype(BF16)
        o_ref[...] = jnp.zeros_like(o_ref)

    wg = wg_ref[...].astype(BF16)
    wv = wv_ref[...].astype(BF16)
    wd = wd_ref[...].astype(BF16)

    def produce(r):
        _, lo, hi = _block_rows(r, rb, tm, seq_len)
        h = h_ref[lo:hi, :]
        return _dot(h, wg), _dot(h, wv)

    def consume(r, up):
        r0, lo, _ = _block_rows(r, rb, tm, seq_len)
        ug = _conv3_block(up[0], cwg_ref, r0, lo, seq_len, rb) + cbg_ref[...]
        uv = _conv3_block(up[1], cwv_ref, r0, lo, seq_len, rb) + cbv_ref[...]
        o_ref[r0:r0 + rb, :] += _dot((_silu(ug) * uv).astype(BF16), wd)

    _skewed(tm // rb, produce, consume)

    @pl.when(j == pl.num_programs(1) - 1)
    def _():
        y = x_ref[...] + g2_ref[0] * o_ref[...]
        o_ref[...] = y * lax.rsqrt(jnp.mean(y * y, axis=-1, keepdims=True) + NORM_EPS) * fw_ref[...]


def _ffn(x1, sh2, sc2, g2, wup, cw, cb, wdown, fw, *, n, tf, rb):
    m, d = x1.shape
    dff = wdown.shape[0]
    nj = dff // tf
    mod_map = lambda i, j: (i, 0, 0)
    return pl.pallas_call(
        functools.partial(_ffn_kernel, seq_len=n, rb=min(rb, n)),
        grid=(m // n, nj),
        in_specs=[
            pl.BlockSpec((n, d), lambda i, j: (i, 0)),
            pl.BlockSpec((1, 1, d), mod_map),
            pl.BlockSpec((1, 1, d), mod_map),
            pl.BlockSpec((1, 1, d), mod_map),
            pl.BlockSpec((d, tf), lambda i, j: (0, j)),
            pl.BlockSpec((d, tf), lambda i, j: (0, j + nj)),
            pl.BlockSpec((3, tf), lambda i, j: (0, j)),
            pl.BlockSpec((3, tf), lambda i, j: (0, j + nj)),
            pl.BlockSpec((1, tf), lambda i, j: (0, j)),
            pl.BlockSpec((1, tf), lambda i, j: (0, j + nj)),
            pl.BlockSpec((tf, d), lambda i, j: (j, 0)),
            pl.BlockSpec((1, d), lambda i, j: (0, 0)),
        ],
        out_specs=pl.BlockSpec((n, d), lambda i, j: (i, 0)),
        out_shape=jax.ShapeDtypeStruct((m, d), F32),
        scratch_shapes=[pltpu.VMEM((n, d), BF16)],
        compiler_params=pltpu.CompilerParams(dimension_semantics=("arbitrary", "arbitrary"),
                                             vmem_limit_bytes=VMEM_LIMIT),
    )(x1, sh2, sc2, g2, wup, wup, cw, cw, cb, cb, wdown, fw)


def _rope_tables(n):
    rows = n // GRID_W
    row_ids = jnp.broadcast_to(jnp.arange(rows, dtype=F32)[:, None], (rows, GRID_W)).reshape(n)
    col_ids = jnp.broadcast_to(jnp.arange(GRID_W, dtype=F32)[None, :], (rows, GRID_W)).reshape(n)
    inv_freq = ROPE_THETA ** (-jnp.arange(0, ROPE_AXIS_DIM, 2, dtype=F32) / ROPE_AXIS_DIM)
    ar = row_ids[:, None] * inv_freq
    ac = col_ids[:, None] * inv_freq
    cos = jnp.concatenate([jnp.cos(ar), jnp.cos(ar), jnp.cos(ac), jnp.cos(ac)], axis=-1)
    sin = jnp.concatenate([-jnp.sin(ar), jnp.sin(ar), -jnp.sin(ac), jnp.sin(ac)], axis=-1)
    return cos, sin


def _gained_rope_tables(cos, sin, gain, scale):
    swapped_gain = gain.reshape(-1, 2, ROPE_AXIS_DIM // 2)[:, ::-1].reshape(gain.shape)
    return cos * (gain * scale), sin * (swapped_gain * scale)


def kernel(x, c, ctx, c_ctx, w_mod, b_mod, w_in, q_norm_w, k_norm_w, conv_qkv_w, a_log, dt_bias, gdn_norm_w, w_pa,
           w_pd, w_out, w_up, ffn_conv_w, ffn_conv_b, w_down, final_norm_w):
    assert w_mod.shape[0] == 1, "single-layer block"
    b, n, d = x.shape
    cl = ctx.shape[1]
    hb = 4

    pad = (-(b + 1)) % 8
    cc = jnp.concatenate([c, c_ctx[None, :], jnp.zeros((pad, d), F32)], axis=0)
    mod = _modulation(cc, w_mod[0], b_mod[0][None, :])
    sh1, sc1, g1, sh2, sc2, g2 = [t[:, None, :] for t in jnp.split(mod[:b], 6, axis=-1)]
    csh1, csc1 = mod[b:b + 1, None, :d], mod[b:b + 1, None, d:2 * d]

    w = w_in[0]
    akv, qkv_w = ATTN_KV_HEADS * HEAD_DIM, 3 * GDN_HEADS * GDN_DK
    o_qkv = 2 * akv
    o_db = o_qkv + qkv_w
    o_aq = o_db + 4 * GDN_HEADS
    w16 = w.astype(BF16)
    w_behind = w16[:, o_aq:]
    w_small = jnp.pad(w16[:, o_db:o_aq], ((0, 0), (0, LANES - 4 * GDN_HEADS)))
    cos, sin = _rope_tables(n)
    qn, kn = q_norm_w, k_norm_w
    tables = _gained_rope_tables(cos, sin, qn[0], SCORE_SCALE) + _gained_rope_tables(cos, sin, kn[0], 1.0)
    x2 = x.reshape(b * n, d)
    common = (w16, w_behind, w_small, qn, kn, conv_qkv_w[0], tables)
    p_lat, s_lat = _input_projection(x2, sh1, sc1, *common, tm=n, seq_len=n, tile0=0,
                                     ntiles=MAIN_COLS // PROJ_TILE, rope=True)
    p_ctx, s_ctx = _input_projection(ctx.reshape(b * cl, d), csh1, csc1, *common, tm=b * cl, seq_len=cl,
                                     tile0=0, ntiles=CTX_COLS // PROJ_TILE, rope=False)

    ncc, ncl = cl // GDN_CHUNK, n // GDN_CHUNK
    nc = ncc + ncl
    ng = GDN_HEADS // hb
    nsm = 4 * GDN_HEADS
    small = jnp.concatenate([s_ctx[:, :nsm].reshape(b, cl, nsm), s_lat[:, :nsm].reshape(b, n, nsm)], axis=1)
    small = small.reshape(b, nc, GDN_CHUNK, 2, 2, ng, hb).transpose(0, 3, 5, 1, 4, 6, 2)
    small = small.reshape(b, 2, ng * nc * 2 * hb, GDN_CHUNK)

    def per_row(p):
        p = jnp.broadcast_to(p.reshape(2, ng, 1, hb).transpose(1, 2, 0, 3), (ng, nc, 2, hb))
        return p.reshape(-1, 1)

    gpack = _gates(small, per_row(a_log[0]), per_row(dt_bias[0]), ng=ng, nc=nc, hb=hb)

    attn = _attention(p_lat, p_ctx, b=b, n=n, cl=cl, tq=min(n, 1024))
    o_f, o_b = _gdn(p_lat, p_ctx, gpack, b=b, n=n, cl=cl, hb=hb, cu=4)

    x1 = _merge(x2, attn, o_f, o_b, p_lat, g1, gdn_norm_w, w_pa[0].astype(BF16), w_pd[0].astype(BF16),
                w_out[0].astype(BF16), n=n, tm=min(n, 512))

    out = _ffn(x1, sh2, sc2, g2, w_up[0], ffn_conv_w[0], ffn_conv_b[0][None, :], w_down[0], final_norm_w[None, :],
               n=n, tf=256, rb=4 * ROW_BLOCK)
    return out.reshape(b, n, d)
```

```python
import functools
import math

import jax
import jax.numpy as jnp
from jax import lax
from jax.experimental import pallas as pl
from jax.experimental.pallas import tpu as pltpu

F32 = jnp.float32
BF16 = jnp.bfloat16

LANES = 128
GRID_W = 64
ATTN_HEADS = 8
ATTN_KV_HEADS = 2
ATTN_GROUP = ATTN_HEADS // ATTN_KV_HEADS
HEAD_DIM = 128
ROPE_AXIS_DIM = HEAD_DIM // 2
ROPE_THETA = 10000.0
GDN_HEADS = 8
GDN_DK = 128
GDN_DV = 128
GDN_CHUNK = 128
NORM_EPS = 1e-6
SCORE_SCALE = HEAD_DIM ** -0.5
NEG_BIG = -1e30

PROJ_TILE = 512
UNIT_COLS = 256
COL_AK = 0
COL_AV = 256
COL_GQ = 512
COL_GK = 1536
COL_GV = 2560
COL_AQ = 3584
COL_Z = 4608
COL_GATES = 5632
MAIN_COLS = 7680
CTX_COLS = COL_AQ
VMEM_LIMIT = 56 * 1024 * 1024


def _dot(a, b):
    return jnp.dot(a, b, preferred_element_type=F32)


def _dot_nt(a, b):
    return lax.dot_general(a, b, (((1,), (1,)), ((), ())), preferred_element_type=F32)


def _sigmoid(x):
    return 0.5 * jnp.tanh(0.5 * x) + 0.5


def _silu(x):
    h = 0.5 * x
    return h * jnp.tanh(h) + h


def _mod_kernel(c_ref, w_ref, b_ref, o_ref):
    s = _silu(c_ref[...])
    w = w_ref[...]
    s_hi = s.astype(BF16)
    s_lo = (s - s_hi.astype(F32)).astype(BF16)
    w_hi = w.astype(BF16)
    w_lo = (w - w_hi.astype(F32)).astype(BF16)
    o_ref[...] = _dot(s_hi, w_hi) + _dot(s_lo, w_hi) + _dot(s_hi, w_lo) + b_ref[...]


def _modulation(cc, w_mod, b_mod):
    rows, d = cc.shape
    n = w_mod.shape[1]
    tn = 1024
    return pl.pallas_call(
        _mod_kernel,
        grid=(n // tn,),
        in_specs=[pl.BlockSpec((rows, d), lambda j: (0, 0)),
                  pl.BlockSpec((d, tn), lambda j: (0, j)),
                  pl.BlockSpec((1, tn), lambda j: (0, j))],
        out_specs=pl.BlockSpec((rows, tn), lambda j: (0, j)),
        out_shape=jax.ShapeDtypeStruct((rows, n), F32),
    )(cc, w_mod, b_mod)


ROW_BLOCK = 256
ROW_HALO = 16


def _skewed(n, produce, consume):
    prev = produce(0)
    for r in range(1, n):
        cur = produce(r)
        consume(r - 1, prev)
        prev = cur
    consume(n - 1, prev)


def _block_rows(r, rb, tm, seq_len):
    r0 = r * rb
    halo = ROW_HALO if seq_len > rb else 0
    return r0, max(r0 - halo, 0), min(r0 + rb + halo, tm)


def _conv3_block(ext, w_ref, r0, lo, seq_len, rb):
    assert seq_len % rb == 0
    n_ext = ext.shape[0]
    off = r0 - lo
    prev = pltpu.roll(ext, 1, axis=0)[off:off + rb]
    nxt = pltpu.roll(ext, n_ext - 1, axis=0)[off:off + rb]
    row = lax.broadcasted_iota(jnp.int32, (rb, 1), 0)
    if r0 % seq_len == 0:
        prev = jnp.where(row == 0, 0.0, prev)
    if (r0 + rb) % seq_len == 0:
        nxt = jnp.where(row == rb - 1, 0.0, nxt)
    return prev * w_ref[0:1, :] + ext[off:off + rb] * w_ref[1:2, :] + nxt * w_ref[2:3, :]


def _rope(a, cos, sin):
    lane = lax.broadcasted_iota(jnp.int32, a.shape, 1)
    swapped = jnp.where((lane & 32) == 0, pltpu.roll(a, 96, axis=1), pltpu.roll(a, 32, axis=1))
    return a * cos + swapped * sin


def _inproj_kernel(x_ref, sh_ref, sc_ref, wa_ref, wb_ref, ws_ref, qn_ref, kn_ref, cw_ref, qc_ref, qs_ref, kc_ref, ks_ref,
                   o_ref, os_ref, hx_ref, *, tile0, ntiles, seq_len, rope, rb):
    j = pl.program_id(1)
    tm = hx_ref.shape[0]
    assert tm % rb == 0

    @pl.when(j == 0)
    def _():
        x = x_ref[...]
        r = lax.rsqrt(jnp.mean(x * x, axis=-1, keepdims=True) + NORM_EPS)
        h = (x * r) * (1.0 + sc_ref[0]) + sh_ref[0]
        hb = h.astype(BF16)
        hx_ref[...] = hb
        os_ref[...] = _dot(hb, ws_ref[...])

    kind = j + tile0
    nh = UNIT_COLS // LANES

    def head(a, hh):
        return a[:, hh * LANES:(hh + 1) * LANES]

    def store(r0, c0, a):
        o_ref[r0:r0 + rb, c0:c0 + a.shape[1]] = a.astype(BF16)

    def qk_norm_rope(a, w_ref, cos_ref, sin_ref, scale, r0):
        r = lax.rsqrt(jnp.mean(a * a, axis=-1, keepdims=True) + NORM_EPS)
        if rope:
            return r * _rope(a, cos_ref[r0:r0 + rb, :], sin_ref[r0:r0 + rb, :])
        return a * (r * scale) * w_ref[...]

    def ep_gates(acc, r0, lo, c0):
        store(r0, c0, _sigmoid(acc))

    def ep_attn_q(acc, r0, lo, c0):
        for hh in range(nh):
            store(r0, c0 + hh * LANES, qk_norm_rope(head(acc, hh), qn_ref, qc_ref, qs_ref, SCORE_SCALE, r0))

    def ep_plain(acc, r0, lo, c0):
        store(r0, c0, acc)

    def conv_silu(ext, r0, lo, c0):
        return _silu(_conv3_block(ext, cw_ref.at[:, c0:c0 + UNIT_COLS], r0, lo, seq_len, rb))

    def ep_gdn_qk(scale):
        def ep(ext, r0, lo, c0):
            y = conv_silu(ext, r0, lo, c0)
            for hh in range(nh):
                a = head(y, hh)
                store(r0, c0 + hh * LANES, a * (lax.rsqrt(jnp.sum(a * a, axis=-1, keepdims=True) + NORM_EPS) * scale))
        return ep

    def ep_gdn_v(ext, r0, lo, c0):
        store(r0, c0, conv_silu(ext, r0, lo, c0))

    def ep_attn_kv(acc, r0, lo, c0):
        if c0 >= ATTN_KV_HEADS * LANES:
            return store(r0, c0, acc)
        for hh in range(nh):
            store(r0, c0 + hh * LANES, qk_norm_rope(head(acc, hh), kn_ref, kc_ref, ks_ref, 1.0, r0))

    bounds = [c // PROJ_TILE for c in (COL_AK, COL_GQ, COL_GK, COL_GV, COL_AQ, COL_Z, COL_GATES, MAIN_COLS)]
    eps = [(ep_attn_kv, False), (ep_gdn_qk(GDN_DK ** -0.5), True), (ep_gdn_qk(1.0), True), (ep_gdn_v, True),
           (ep_attn_q, False), (ep_plain, False), (ep_gates, False)]
    groups = [(lo, hi) + ep for lo, hi, ep in zip(bounds[:-1], bounds[1:], eps)]
    ncol = PROJ_TILE // UNIT_COLS

    for k_lo, k_hi, epilogue, conv in groups:
        if max(k_lo, tile0) >= min(k_hi, tile0 + ntiles):
            continue

        w_ref = wa_ref if k_lo < CTX_COLS // PROJ_TILE else wb_ref

        def run(epilogue=epilogue, conv=conv, w_ref=w_ref):
            def rows(u):
                r = u // ncol
                return _block_rows(r, rb, tm, seq_len) if conv else (r * rb, r * rb, (r + 1) * rb)

            def produce(u):
                _, lo, hi = rows(u)
                c0 = (u % ncol) * UNIT_COLS
                return _dot(hx_ref[lo:hi, :], w_ref[:, c0:c0 + UNIT_COLS])

            def consume(u, acc):
                r0, lo, _ = rows(u)
                epilogue(acc, r0, lo, (u % ncol) * UNIT_COLS)

            _skewed((tm // rb) * ncol, produce, consume)

        pl.when((kind >= k_lo) & (kind < k_hi))(run)


def _input_projection(x2, shift, scale, w_a, w_b, w_small, qn, kn, cw, tables, *, tm, seq_len, tile0, ntiles, rope):
    m, d = x2.shape
    nrow = m // tm
    per_row_mod = shift.shape[0] > 1
    mod_map = (lambda i, j: (i, 0, 0)) if per_row_mod else (lambda i, j: (0, 0, 0))
    conv_tile0 = COL_GQ // PROJ_TILE
    n_conv_tiles = cw.shape[1] // PROJ_TILE
    na, nb = CTX_COLS // PROJ_TILE, w_b.shape[1] // PROJ_TILE
    kern = functools.partial(_inproj_kernel, tile0=tile0, ntiles=ntiles, seq_len=seq_len, rope=rope,
                             rb=min(2 * ROW_BLOCK, seq_len))
    return pl.pallas_call(
        kern,
        grid=(nrow, ntiles),
        in_specs=[
            pl.BlockSpec((tm, d), lambda i, j: (i, 0)),
            pl.BlockSpec((1, 1, d), mod_map),
            pl.BlockSpec((1, 1, d), mod_map),
            pl.BlockSpec((d, PROJ_TILE), lambda i, j: (0, jnp.minimum(j + tile0, na - 1))),
            pl.BlockSpec((d, PROJ_TILE), lambda i, j: (0, jnp.clip(j + tile0 - na, 0, nb - 1))),
            pl.BlockSpec((d, LANES), lambda i, j: (0, 0)),
            pl.BlockSpec((1, LANES), lambda i, j: (0, 0)),
            pl.BlockSpec((1, LANES), lambda i, j: (0, 0)),
            pl.BlockSpec((3, PROJ_TILE), lambda i, j: (0, jnp.clip(j + tile0 - conv_tile0, 0, n_conv_tiles - 1))),
        ] + [pl.BlockSpec(t.shape, lambda i, j: (0, 0), pipeline_mode=pl.Buffered(1)) for t in tables],
        out_specs=[pl.BlockSpec((tm, PROJ_TILE), lambda i, j: (i, j)),
                   pl.BlockSpec((tm, LANES), lambda i, j: (i, 0))],
        out_shape=[jax.ShapeDtypeStruct((m, ntiles * PROJ_TILE), BF16),
                   jax.ShapeDtypeStruct((m, LANES), F32)],
        scratch_shapes=[pltpu.VMEM((tm, d), BF16)],
        compiler_params=pltpu.CompilerParams(dimension_semantics=("arbitrary", "arbitrary"),
                                             vmem_limit_bytes=VMEM_LIMIT),
    )(x2, shift, scale, w_a, w_b, w_small, qn, kn, cw, *tables)


def _gate_kernel(g_ref, alog_ref, dtb_ref, o_ref, *, hb):
    ng, nc = o_ref.shape[1], o_ref.shape[2]
    assert 2 * hb == 8, "one (direction, head) group per sublane tile"
    db = g_ref[0, 0]
    da = g_ref[0, 1]
    beta = _sigmoid(db)
    z = da + dtb_ref[...]
    softplus = jnp.maximum(z, 0.0) + jnp.log(1.0 + jnp.exp(-jnp.abs(z)))
    la = -jnp.exp(alog_ref[...]) * softplus
    lane = lax.broadcasted_iota(jnp.int32, la.shape, 1)
    backward = (lax.broadcasted_iota(jnp.int32, la.shape, 0) // hb) % 2 == 1
    pre = la
    suf = la
    k = 1
    while k < LANES:
        pre = pre + jnp.where(lane >= k, pltpu.roll(pre, k, axis=1), 0.0)
        suf = suf + jnp.where(lane < LANES - k, pltpu.roll(suf, LANES - k, axis=1), 0.0)
        k *= 2
    gam = jnp.where(backward, suf, pre)
    tot = jnp.sum(la, axis=-1, keepdims=True)
    eg = jnp.exp(gam)
    vals = (beta, gam, eg, beta * eg, jnp.exp(tot - gam), jnp.broadcast_to(jnp.exp(tot), gam.shape))
    for qi, v in enumerate(vals):
        o_ref[0, :, :, qi] = v.reshape(ng, nc, 2 * hb, LANES)


def _gates(g, alog_rows, dtb_rows, *, ng, nc, hb):
    b, _, rows, _ = g.shape
    return pl.pallas_call(
        functools.partial(_gate_kernel, hb=hb),
        grid=(b,),
        in_specs=[pl.BlockSpec((1, 2, rows, LANES), lambda i: (i, 0, 0, 0)),
                  pl.BlockSpec((rows, 1), lambda i: (0, 0)),
                  pl.BlockSpec((rows, 1), lambda i: (0, 0))],
        out_specs=pl.BlockSpec((1, ng, nc, 6, 2 * hb, LANES), lambda i: (i, 0, 0, 0, 0, 0)),
        out_shape=jax.ShapeDtypeStruct((b, ng, nc, 6, 2 * hb, LANES), F32),
    )(g, alog_rows, dtb_rows)


def _attn_kernel(q_ref, kc_ref, vc_ref, kx_ref, vx_ref, o_ref):
    kc = kc_ref[...]
    kx = kx_ref[...]
    vc = jnp.concatenate([vc_ref[...], jnp.ones(vc_ref.shape, BF16)], axis=1)
    vx = jnp.concatenate([vx_ref[...], jnp.ones(vx_ref.shape, BF16)], axis=1)
    tq = q_ref.shape[0]
    rq = min(tq, 2 * ROW_BLOCK)
    nblk = tq // rq

    def unit(u):
        hh, blk = divmod(u, nblk)
        return slice(blk * rq, (blk + 1) * rq), slice(hh * LANES, (hh + 1) * LANES)

    def scores(u):
        rows, cols = unit(u)
        q = q_ref[rows, cols]
        return _dot_nt(q, kc), _dot_nt(q, kx)

    def softmax_pv(u, s):
        rows, cols = unit(u)
        sc, sx = s
        m = jnp.maximum(jnp.max(sc, axis=-1, keepdims=True), jnp.max(sx, axis=-1, keepdims=True))
        pc = jnp.exp(sc - m).astype(BF16)
        px = jnp.exp(sx - m).astype(BF16)
        o = _dot(pc, vc) + _dot(px, vx)
        o_ref[rows, cols] = (o[:, :LANES] / o[:, LANES:LANES + 1]).astype(BF16)

    _skewed(ATTN_GROUP * nblk, scores, softmax_pv)


def _attention(p_lat, p_ctx, *, b, n, cl, tq):
    gw = ATTN_GROUP * LANES
    nq = n // tq
    q0 = COL_AQ // gw
    kx0, vx0 = COL_AK // LANES, COL_AV // LANES
    return pl.pallas_call(
        _attn_kernel,
        grid=(b, ATTN_KV_HEADS, nq),
        in_specs=[
            pl.BlockSpec((tq, gw), lambda i, h, t: (i * nq + t, q0 + h)),
            pl.BlockSpec((cl, LANES), lambda i, h, t: (i, kx0 + h)),
            pl.BlockSpec((cl, LANES), lambda i, h, t: (i, vx0 + h)),
            pl.BlockSpec((n, LANES), lambda i, h, t: (i, kx0 + h)),
            pl.BlockSpec((n, LANES), lambda i, h, t: (i, vx0 + h)),
        ],
        out_specs=pl.BlockSpec((tq, gw), lambda i, h, t: (i * nq + t, h)),
        out_shape=jax.ShapeDtypeStruct((b * n, ATTN_HEADS * LANES), BF16),
        compiler_params=pltpu.CompilerParams(dimension_semantics=("arbitrary", "arbitrary", "arbitrary"),
                                             vmem_limit_bytes=VMEM_LIMIT),
    )(p_lat, p_ctx, p_ctx, p_lat, p_lat)


G_BETA, G_GAM, G_EG, G_BEG, G_EGD, G_GL = range(6)


def _gdn_kernel(ql_ref, kl_ref, vl_ref, kc_ref, vc_ref, g_ref, of_ref, ob_ref, mq_ref, bo_ref, s_ref,
                *, hb, ncc, ncl, cu):
    c_sz = GDN_CHUNK
    ri = lax.broadcasted_iota(jnp.int32, (c_sz, c_sz), 0)
    ci = lax.broadcasted_iota(jnp.int32, (c_sz, c_sz), 1)
    eye = (ri == ci).astype(F32)
    levels = int(math.log2(c_sz))
    cu_ctx = math.gcd(ncc, cu)

    def tri_masks(d):
        lo, hi = (ci, ri) if d == 0 else (ri, ci)
        incl = hi >= lo
        strict = hi > lo
        pair = []
        for lv in range(levels):
            bh = hi >> lv
            bl = lo >> lv
            pair.append(((bh & 1) == 1) & (bl == bh - 1))
        return incl, strict, pair

    masks = [tri_masks(0), tri_masks(1)]

    def grow(gt, qi, d, hh):
        return gt[qi, d * hb + hh:d * hb + hh + 1, :]

    def prep(chunks):
        inst = []
        for c, k, v, q in chunks:
            rq = pl.multiple_of(c * (2 * c_sz), 2 * c_sz)
            gt = g_ref[0, 0, c]
            gam_rows = gt[G_GAM]
            gam_cols = jnp.concatenate([gam_rows, jnp.zeros((c_sz - 2 * hb, c_sz), F32)], axis=0).T
            for hh in range(hb):
                kh = k[hh]
                kk = _dot_nt(kh, kh)
                qk = None if q is None else _dot_nt(q[hh], kh)
                kt = kh.astype(F32).T
                for d in range(2):
                    inst.append(dict(rq=rq, gt=gt, hh=hh, d=d, hd=hh * 2 + d, kh=kh, vh=v[hh], kk=kk, qk=qk, kt=kt,
                                     qh=None if q is None else q[hh],
                                     gam_c=gam_cols[:, d * hb + hh:d * hb + hh + 1]))
        for it in inst:
            d, hh, gt = it["d"], it["hh"], it["gt"]
            incl, strict, pair = masks[d]
            beta_r = grow(gt, G_BETA, d, hh)
            e = jnp.exp(jnp.where(incl, it["gam_c"] - grow(gt, G_GAM, d, hh), NEG_BIG))
            l2 = jnp.where(strict, e * it["kk"], 0.0) * beta_r
            it["t"] = eye - jnp.where(pair[0], l2, 0.0)
            it["ms"] = [jnp.where(pair[lv], l2, 0.0).astype(BF16) for lv in range(1, levels)]
            it["kb"] = (it["kt"] * (grow(gt, G_EGD, d, hh) * beta_r)).astype(BF16)
            if it["qk"] is not None:
                pb = (e * it["qk"] * beta_r).astype(BF16)
                it["pb"] = pb
                it["qlhs"] = jnp.concatenate([(eye * grow(gt, G_EG, d, hh)).astype(BF16), -pb], axis=1)
        for lv in range(levels - 1):
            xs = [_dot(it["ms"][lv], it["t"].astype(BF16)).astype(BF16) for it in inst]
            for it, xm in zip(inst, xs):
                it["t"] = it["t"] - _dot(it["t"].astype(BF16), xm)
        for it in inst:
            it["yu"] = _dot(it["t"].astype(BF16), it["vh"]).astype(BF16)
        for it in inst:
            it["yw"] = _dot((it["t"] * grow(it["gt"], G_EG, it["d"], it["hh"])).astype(BF16), it["kh"]).astype(BF16)
        for it in inst:
            bm = _dot(it["kb"], jnp.concatenate([it["yu"], it["yw"]], axis=1))
            mq_ref[it["hd"], pl.ds(it["rq"], c_sz), :] = (-bm[:, GDN_DV:]).astype(BF16)
            bo_ref[it["hd"], pl.ds(it["rq"], c_sz), :] = bm[:, :GDN_DV].astype(BF16)
        for it in inst:
            if it["qk"] is not None:
                oo = _dot(it["pb"], it["yu"])
                qq = _dot(it["qlhs"], jnp.concatenate([it["qh"], it["yw"]], axis=0))
                mq_ref[it["hd"], pl.ds(it["rq"] + c_sz, c_sz), :] = qq.astype(BF16)
                bo_ref[it["hd"], pl.ds(it["rq"] + c_sz, c_sz), :] = oo.astype(BF16)

    def split_heads(ref, r0):
        return [ref[pl.ds(r0, c_sz), hh * LANES:(hh + 1) * LANES] for hh in range(hb)]

    def prep_ctx(i, carry):
        chunks = []
        for j in range(cu_ctx):
            c = i * cu_ctx + j
            r0 = pl.multiple_of(c * c_sz, c_sz)
            chunks.append((c, split_heads(kc_ref, r0), split_heads(vc_ref, r0), None))
        prep(chunks)
        return carry

    def prep_lat_pair(i):
        chunks = []
        for c in (i, ncl - 1 - i):
            r0 = pl.multiple_of(c * c_sz, c_sz)
            chunks.append((c + ncc, split_heads(kl_ref, r0), split_heads(vl_ref, r0), split_heads(ql_ref, r0)))
        prep(chunks)

    def scan_steps(steps):
        st = []
        for hh, d, c, out_row in steps:
            hd = hh * 2 + d
            rq = pl.multiple_of(c * (2 * c_sz), 2 * c_sz)
            rows = c_sz if out_row is None else 2 * c_sz
            s = s_ref[hd]
            st.append(dict(hd=hd, hh=hh, d=d, s=s, rq=rq, rows=rows, out_row=out_row,
                           gl=g_ref[0, 0, c, G_GL, pl.ds(d * hb + hh, 1), :],
                           x=_dot(mq_ref[hd, pl.ds(rq, rows), :], s.astype(BF16))))
        for it in st:
            bo = bo_ref[it["hd"], pl.ds(it["rq"], it["rows"]), :].astype(F32)
            s_ref[it["hd"]] = it["gl"] * it["s"] + it["x"][:c_sz] + bo[:c_sz]
            if it["out_row"] is not None:
                out_ref = of_ref if it["d"] == 0 else ob_ref
                out_ref[pl.ds(it["out_row"], c_sz), it["hh"] * LANES:(it["hh"] + 1) * LANES] = (
                    it["x"][c_sz:] + bo[c_sz:]).astype(BF16)

    def scan_ctx(i, carry):
        steps = []
        for hh in range(hb):
            steps += [(hh, 0, i, None), (hh, 1, ncc - 1 - i, None)]
        scan_steps(steps)
        return carry

    def scan_lat(i):
        steps = []
        for hh in range(hb):
            steps += [(hh, 0, ncc + i, pl.multiple_of(i * c_sz, c_sz)),
                      (hh, 1, ncc + ncl - 1 - i, pl.multiple_of((ncl - 1 - i) * c_sz, c_sz))]
        scan_steps(steps)

    assert ncl % 2 == 0
    lax.fori_loop(0, ncc // cu_ctx, prep_ctx, 0)
    s_ref[...] = jnp.zeros_like(s_ref)
    prep_lat_pair(0)
    lax.fori_loop(0, ncc, scan_ctx, 0)

    def scan_and_prep(i, carry):
        scan_lat(i)
        prep_lat_pair(i + 1)
        return carry

    def scan_only(i, carry):
        scan_lat(i)
        return carry

    lax.fori_loop(0, ncl // 2 - 1, scan_and_prep, 0)
    lax.fori_loop(ncl // 2 - 1, ncl, scan_only, 0)


def _gdn(p_lat, p_ctx, gpack, *, b, n, cl, hb, cu):
    gw = hb * LANES
    ng = GDN_HEADS // hb
    ncc, ncl = cl // GDN_CHUNK, n // GDN_CHUNK
    nc = ncc + ncl
    q0, k0, v0 = COL_GQ // gw, COL_GK // gw, COL_GV // gw
    kern = functools.partial(_gdn_kernel, hb=hb, ncc=ncc, ncl=ncl, cu=cu)
    out = jax.ShapeDtypeStruct((b * n, GDN_HEADS * LANES), BF16)
    return pl.pallas_call(
        kern,
        grid=(b, ng),
        in_specs=[
            pl.BlockSpec((n, gw), lambda i, g: (i, q0 + g)),
            pl.BlockSpec((n, gw), lambda i, g: (i, k0 + g)),
            pl.BlockSpec((n, gw), lambda i, g: (i, v0 + g)),
            pl.BlockSpec((cl, gw), lambda i, g: (i, k0 + g)),
            pl.BlockSpec((cl, gw), lambda i, g: (i, v0 + g)),
            pl.BlockSpec((1, 1) + gpack.shape[2:], lambda i, g: (i, g, 0, 0, 0, 0)),
        ],
        out_specs=[pl.BlockSpec((n, gw), lambda i, g: (i, g)), pl.BlockSpec((n, gw), lambda i, g: (i, g))],
        out_shape=[out, out],
        scratch_shapes=[
            pltpu.VMEM((2 * hb, 2 * nc * GDN_CHUNK, LANES), BF16),
            pltpu.VMEM((2 * hb, 2 * nc * GDN_CHUNK, LANES), BF16),
            pltpu.VMEM((2 * hb, GDN_DK, GDN_DV), F32),
        ],
        compiler_params=pltpu.CompilerParams(dimension_semantics=("arbitrary", "arbitrary"),
                                             vmem_limit_bytes=VMEM_LIMIT),
    )(p_lat, p_lat, p_lat, p_ctx, p_ctx, gpack)


def _merge_kernel(x_ref, a_ref, of_ref, ob_ref, z0_ref, z1_ref, ga0_ref, ga1_ref, gd0_ref, gd1_ref, g1_ref, nw_ref,
                  wpa_ref, wpd_ref, wout_ref, o_ref):
    ya = _dot(a_ref[...], wpa_ref[...])
    heads = []
    per_half = GDN_HEADS // 2
    for hh in range(GDN_HEADS):
        cols = slice(hh * LANES, (hh + 1) * LANES)
        zcols = slice((hh % per_half) * LANES, (hh % per_half + 1) * LANES)
        z = (z0_ref if hh < per_half else z1_ref)[:, zcols].astype(F32)
        o = of_ref[:, cols].astype(F32) + ob_ref[:, cols].astype(F32)
        o = o * lax.rsqrt(jnp.mean(o * o, axis=-1, keepdims=True) + NORM_EPS) * nw_ref[...]
        heads.append((o * _silu(z)).astype(BF16))
    yd = _dot(jnp.concatenate(heads, axis=1), wpd_ref[...])
    ga = jnp.concatenate([ga0_ref[...], ga1_ref[...]], axis=1).astype(F32)
    gd = jnp.concatenate([gd0_ref[...], gd1_ref[...]], axis=1).astype(F32)
    y = ga * ya + gd * yd
    o_ref[...] = x_ref[...] + g1_ref[0] * _dot(y.astype(BF16), wout_ref[...])


def _merge(x2, attn, o_f, o_b, p_lat, g1, nw, wpa, wpd, wout, *, n, tm):
    m, d = x2.shape
    per_seq = n // tm
    row = lambda i: (i, 0)
    const = lambda i: (0, 0)
    return pl.pallas_call(
        _merge_kernel,
        grid=(m // tm,),
        in_specs=[
            pl.BlockSpec((tm, d), row),
            pl.BlockSpec((tm, d), row),
            pl.BlockSpec((tm, d), row),
            pl.BlockSpec((tm, d), row),
        ] + [pl.BlockSpec((tm, PROJ_TILE), functools.partial(lambda i, t: (i, t), t=c0 // PROJ_TILE + k))
             for c0 in (COL_Z, COL_GATES, COL_GATES + d) for k in range(2)] + [
            pl.BlockSpec((1, 1, d), lambda i: (i // per_seq, 0, 0)),
            pl.BlockSpec((1, LANES), const),
            pl.BlockSpec((d, d), const),
            pl.BlockSpec((d, d), const),
            pl.BlockSpec((d, d), const),
        ],
        out_specs=pl.BlockSpec((tm, d), row),
        out_shape=jax.ShapeDtypeStruct((m, d), F32),
        compiler_params=pltpu.CompilerParams(dimension_semantics=("arbitrary",), vmem_limit_bytes=VMEM_LIMIT),
    )(x2, attn, o_f, o_b, *([p_lat] * 6), g1, nw, wpa, wpd, wout)


def _ffn_kernel(x_ref, sh_ref, sc_ref, g2_ref, wg_ref, wv_ref, cwg_ref, cwv_ref, cbg_ref, cbv_ref, wd_ref, fw_ref,
                o_ref, h_ref, *, seq_len, rb):
    j = pl.program_id(1)
    tm = h_ref.shape[0]
    assert tm % rb == 0

    @pl.when(j == 0)
    def _():
        x = x_ref[...]
        r = lax.rsqrt(jnp.mean(x * x, axis=-1, keepdims=True) + NORM_EPS)
        h_ref[...] = ((x * r) * (1.0 + sc_ref[0]) + sh_ref[0]).astype(BF16)
        o_ref[...] = jnp.zeros_like(o_ref)

    wg = wg_ref[...].astype(BF16)
    wv = wv_ref[...].astype(BF16)
    wd = wd_ref[...].astype(BF16)

    def produce(r):
        _, lo, hi = _block_rows(r, rb, tm, seq_len)
        h = h_ref[lo:hi, :]
        return _dot(h, wg), _dot(h, wv)

    def consume(r, up):
        r0, lo, _ = _block_rows(r, rb, tm, seq_len)
        ug = _conv3_block(up[0], cwg_ref, r0, lo, seq_len, rb) + cbg_ref[...]
        uv = _conv3_block(up[1], cwv_ref, r0, lo, seq_len, rb) + cbv_ref[...]
        o_ref[r0:r0 + rb, :] += _dot((_silu(ug) * uv).astype(BF16), wd)

    _skewed(tm // rb, produce, consume)

    @pl.when(j == pl.num_programs(1) - 1)
    def _():
        y = x_ref[...] + g2_ref[0] * o_ref[...]
        o_ref[...] = y * lax.rsqrt(jnp.mean(y * y, axis=-1, keepdims=True) + NORM_EPS) * fw_ref[...]


def _ffn(x1, sh2, sc2, g2, wup, cw, cb, wdown, fw, *, n, tf, rb):
    m, d = x1.shape
    dff = wdown.shape[0]
    nj = dff // tf
    mod_map = lambda i, j: (i, 0, 0)
    return pl.pallas_call(
        functools.partial(_ffn_kernel, seq_len=n, rb=min(rb, n)),
        grid=(m // n, nj),
        in_specs=[
            pl.BlockSpec((n, d), lambda i, j: (i, 0)),
            pl.BlockSpec((1, 1, d), mod_map),
            pl.BlockSpec((1, 1, d), mod_map),
            pl.BlockSpec((1, 1, d), mod_map),
            pl.BlockSpec((d, tf), lambda i, j: (0, j)),
            pl.BlockSpec((d, tf), lambda i, j: (0, j + nj)),
            pl.BlockSpec((3, tf), lambda i, j: (0, j)),
            pl.BlockSpec((3, tf), lambda i, j: (0, j + nj)),
            pl.BlockSpec((1, tf), lambda i, j: (0, j)),
            pl.BlockSpec((1, tf), lambda i, j: (0, j + nj)),
            pl.BlockSpec((tf, d), lambda i, j: (j, 0)),
            pl.BlockSpec((1, d), lambda i, j: (0, 0)),
        ],
        out_specs=pl.BlockSpec((n, d), lambda i, j: (i, 0)),
        out_shape=jax.ShapeDtypeStruct((m, d), F32),
        scratch_shapes=[pltpu.VMEM((n, d), BF16)],
        compiler_params=pltpu.CompilerParams(dimension_semantics=("arbitrary", "arbitrary"),
                                             vmem_limit_bytes=VMEM_LIMIT),
    )(x1, sh2, sc2, g2, wup, wup, cw, cw, cb, cb, wdown, fw)


def _rope_tables(n):
    rows = n // GRID_W
    row_ids = jnp.broadcast_to(jnp.arange(rows, dtype=F32)[:, None], (rows, GRID_W)).reshape(n)
    col_ids = jnp.broadcast_to(jnp.arange(GRID_W, dtype=F32)[None, :], (rows, GRID_W)).reshape(n)
    inv_freq = ROPE_THETA ** (-jnp.arange(0, ROPE_AXIS_DIM, 2, dtype=F32) / ROPE_AXIS_DIM)
    ar = row_ids[:, None] * inv_freq
    ac = col_ids[:, None] * inv_freq
    cos = jnp.concatenate([jnp.cos(ar), jnp.cos(ar), jnp.cos(ac), jnp.cos(ac)], axis=-1)
    sin = jnp.concatenate([-jnp.sin(ar), jnp.sin(ar), -jnp.sin(ac), jnp.sin(ac)], axis=-1)
    return cos, sin


def _gained_rope_tables(cos, sin, gain, scale):
    swapped_gain = gain.reshape(-1, 2, ROPE_AXIS_DIM // 2)[:, ::-1].reshape(gain.shape)
    return cos * (gain * scale), sin * (swapped_gain * scale)


def kernel(x, c, ctx, c_ctx, w_mod, b_mod, w_in, q_norm_w, k_norm_w, conv_qkv_w, a_log, dt_bias, gdn_norm_w, w_pa,
           w_pd, w_out, w_up, ffn_conv_w, ffn_conv_b, w_down, final_norm_w):
    assert w_mod.shape[0] == 1, "single-layer block"
    b, n, d = x.shape
    cl = ctx.shape[1]
    hb = 4

    pad = (-(b + 1)) % 8
    cc = jnp.concatenate([c, c_ctx[None, :], jnp.zeros((pad, d), F32)], axis=0)
    mod = _modulation(cc, w_mod[0], b_mod[0][None, :])
    sh1, sc1, g1, sh2, sc2, g2 = [t[:, None, :] for t in jnp.split(mod[:b], 6, axis=-1)]
    csh1, csc1 = mod[b:b + 1, None, :d], mod[b:b + 1, None, d:2 * d]

    w = w_in[0]
    akv, qkv_w = ATTN_KV_HEADS * HEAD_DIM, 3 * GDN_HEADS * GDN_DK
    o_qkv = 2 * akv
    o_db = o_qkv + qkv_w
    o_aq = o_db + 4 * GDN_HEADS
    w16 = w.astype(BF16)
    w_behind = w16[:, o_aq:]
    w_small = jnp.pad(w16[:, o_db:o_aq], ((0, 0), (0, LANES - 4 * GDN_HEADS)))
    cos, sin = _rope_tables(n)
    qn, kn = q_norm_w, k_norm_w
    tables = _gained_rope_tables(cos, sin, qn[0], SCORE_SCALE) + _gained_rope_tables(cos, sin, kn[0], 1.0)
    x2 = x.reshape(b * n, d)
    common = (w16, w_behind, w_small, qn, kn, conv_qkv_w[0], tables)
    p_lat, s_lat = _input_projection(x2, sh1, sc1, *common, tm=n, seq_len=n, tile0=0,
                                     ntiles=MAIN_COLS // PROJ_TILE, rope=True)
    p_ctx, s_ctx = _input_projection(ctx.reshape(b * cl, d), csh1, csc1, *common, tm=b * cl, seq_len=cl,
                                     tile0=0, ntiles=CTX_COLS // PROJ_TILE, rope=False)

    ncc, ncl = cl // GDN_CHUNK, n // GDN_CHUNK
    nc = ncc + ncl
    ng = GDN_HEADS // hb
    nsm = 4 * GDN_HEADS
    small = jnp.concatenate([s_ctx[:, :nsm].reshape(b, cl, nsm), s_lat[:, :nsm].reshape(b, n, nsm)], axis=1)
    small = small.reshape(b, nc, GDN_CHUNK, 2, 2, ng, hb).transpose(0, 3, 5, 1, 4, 6, 2)
    small = small.reshape(b, 2, ng * nc * 2 * hb, GDN_CHUNK)

    def per_row(p):
        p = jnp.broadcast_to(p.reshape(2, ng, 1, hb).transpose(1, 2, 0, 3), (ng, nc, 2, hb))
        return p.reshape(-1, 1)

    gpack = _gates(small, per_row(a_log[0]), per_row(dt_bias[0]), ng=ng, nc=nc, hb=hb)

    attn = _attention(p_lat, p_ctx, b=b, n=n, cl=cl, tq=min(n, 1024))
    o_f, o_b = _gdn(p_lat, p_ctx, gpack, b=b, n=n, cl=cl, hb=hb, cu=4)

    x1 = _merge(x2, attn, o_f, o_b, p_lat, g1, gdn_norm_w, w_pa[0].astype(BF16), w_pd[0].astype(BF16),
                w_out[0].astype(BF16), n=n, tm=min(n, 512))

    out = _ffn(x1, sh2, sc2, g2, w_up[0], ffn_conv_w[0], ffn_conv_b[0][None, :], w_down[0], final_norm_w[None, :],
               n=n, tf=256, rb=4 * ROW_BLOCK)
    return out.reshape(b, n, d)
```
